```python
import math
import jax
import jax.numpy as jnp
from jax import lax
import numpy as np

D_MODEL = 1024
BATCH = 8
SEQ = 4096
DEPTH = 4

GRID_W = 64
CTX_LEN = 256
GROUP_W = 256
N_GROUPS = 4
MIX_W = GROUP_W * N_GROUPS
N_IN_SLICES = 14
IN_W = GROUP_W * N_IN_SLICES
LRU_HEADS = 4
LRU_BLOCK = GROUP_W // LRU_HEADS
LRU_CONV = 4
LRU_C = 8.0
HGRN_HEADS = 4
HGRN_HEAD_DIM = GROUP_W // HGRN_HEADS
HGRN_CHUNK = 64
LB_FLOOR = 1e-30
CONV_K = 31
DIFF_HEADS = 4
DIFF_HEAD_DIM = GROUP_W // (2 * DIFF_HEADS)
Q_BLOCK = 128
ROPE_THETA = 10000.0
RMS_EPS = 1e-6
LN_EPS = 1e-5

kernel_name = 'hybrid_parallel_heads_dit_block'


def rms_norm(x, g):
    xf = x.astype(jnp.float32)
    y = xf * lax.rsqrt(jnp.mean(xf * xf, axis=-1, keepdims=True) + RMS_EPS)
    return (y * g.astype(jnp.float32)).astype(x.dtype)


def layer_norm(x, g, b):
    xf = x.astype(jnp.float32)
    mu = jnp.mean(xf, axis=-1, keepdims=True)
    var = jnp.mean(jnp.square(xf - mu), axis=-1, keepdims=True)
    y = (xf - mu) * lax.rsqrt(var + LN_EPS) * g.astype(jnp.float32) + b.astype(jnp.float32)
    return y.astype(x.dtype)


def depthwise_conv(x, w, b, pad_left, pad_right):
    y = lax.conv_general_dilated(
        x, w[:, None, :].astype(x.dtype), window_strides=(1,),
        padding=[(pad_left, pad_right)], dimension_numbers=('NWC', 'WIO', 'NWC'),
        feature_group_count=x.shape[-1])
    return y + b.astype(x.dtype)


def axial_rope_tables(n_tokens):
    rows = n_tokens // GRID_W
    row = jnp.repeat(jnp.arange(rows, dtype=jnp.float32), GRID_W)
    col = jnp.tile(jnp.arange(GRID_W, dtype=jnp.float32), rows)
    n_freq = DIFF_HEAD_DIM // 4
    inv_freq = ROPE_THETA ** (-jnp.arange(n_freq, dtype=jnp.float32) / n_freq)
    ang = jnp.concatenate([row[:, None] * inv_freq, col[:, None] * inv_freq], axis=-1)
    return jnp.cos(ang), jnp.sin(ang)


def apply_rope(x, cos, sin):
    half = x.shape[-1] // 2
    x1, x2 = x[..., :half], x[..., half:]
    c = cos[None, :, None, :].astype(x.dtype)
    s = sin[None, :, None, :].astype(x.dtype)
    return jnp.concatenate([x1 * c - x2 * s, x1 * s + x2 * c], axis=-1)


def block_diag(x, w, b):
    bsz, l, _ = x.shape
    y = jnp.einsum('blhi,hij->blhj', x.reshape(bsz, l, LRU_HEADS, LRU_BLOCK), w)
    return y.reshape(bsz, l, GROUP_W) + b


def linear_scan(a, u, h0):
    def combine(left, right):
        return left[0] * right[0], right[0] * left[1] + right[1]
    a_cum, h = lax.associative_scan(combine, (a, u), axis=1)
    return h + a_cum * h0[:, None, :]


def rglru_direction(x_c, x_l, w_r, b_r, w_i, b_i, lam):
    def coeffs(x):
        xf = x.astype(jnp.float32)
        r = jax.nn.sigmoid(block_diag(xf, w_r, b_r))
        i = jax.nn.sigmoid(block_diag(xf, w_i, b_i))
        log_a = -LRU_C * r * jax.nn.softplus(-lam.astype(jnp.float32))
        return jnp.exp(log_a), jnp.sqrt(jnp.maximum(-jnp.expm1(2.0 * log_a), 0.0)) * (i * xf)
    a_c, u_c = coeffs(x_c)
    h_c = linear_scan(a_c, u_c, jnp.zeros_like(u_c[:, 0]))
    a_l, u_l = coeffs(x_l)
    h_l = linear_scan(a_l, u_l, h_c[:, -1])
    return h_c, h_l


def rglru_mixer(x_l, x_c, conv_w, conv_b, w_r, b_r, w_i, b_i, lam):
    pad_l, pad_r = LRU_CONV // 2, LRU_CONV - 1 - LRU_CONV // 2
    xc_l = depthwise_conv(x_l, conv_w, conv_b, pad_l, pad_r)
    xc_c = depthwise_conv(x_c, conv_w, conv_b, pad_l, pad_r)
    hf_c, hf_l = rglru_direction(xc_c, xc_l, w_r[0], b_r[0], w_i[0], b_i[0], lam[0])
    hb_c, hb_l = rglru_direction(jnp.flip(xc_c, 1), jnp.flip(xc_l, 1), w_r[1], b_r[1], w_i[1], b_i[1], lam[1])
    y_l = (hf_l + jnp.flip(hb_l, 1)).astype(x_l.dtype)
    y_c = (hf_c + jnp.flip(hb_c, 1)).astype(x_c.dtype)
    return y_l, y_c


def log_forget(z, lb):
    zf = z.astype(jnp.float32)
    lbf = lb.astype(jnp.float32)
    return jnp.logaddexp(jnp.log(jnp.maximum(lbf, LB_FLOOR)), jnp.log1p(-lbf) + jax.nn.log_sigmoid(zf))


def gla_chunk_scan(q, k, v, log_f, s0):
    b, l, h, _ = q.shape
    dv = v.shape[-1]
    n = l // HGRN_CHUNK

    def to_chunks(t):
        return t.astype(jnp.float32).reshape(b, n, HGRN_CHUNK, h, t.shape[-1]).transpose(1, 0, 3, 2, 4)

    incl = jnp.tril(jnp.ones((HGRN_CHUNK, HGRN_CHUNK), dtype=bool))[:, :, None]

    def step(state, chunk):
        qc, kc, vc, gc = chunk
        g_cum = jnp.cumsum(gc, axis=2)
        o_inter = jnp.einsum('bhcd,bhde->bhce', qc * jnp.exp(g_cum), state)
        rel = g_cum[:, :, :, None, :] - g_cum[:, :, None, :, :]
        decay = jnp.where(incl, jnp.exp(jnp.minimum(rel, 0.0)), 0.0)
        scores = jnp.einsum('bhid,bhjd,bhijd->bhij', qc, kc, decay)
        o = o_inter + jnp.einsum('bhij,bhje->bhie', scores, vc)
        g_end = g_cum[:, :, -1:, :]
        state = (jnp.exp(g_end[:, :, 0, :, None]) * state
                 + jnp.einsum('bhcd,bhce->bhde', kc * jnp.exp(g_end - g_cum), vc))
        return state, o

    state, o = lax.scan(step, s0.astype(jnp.float32), tuple(to_chunks(t) for t in (q, k, v, log_f)))
    return o.transpose(1, 0, 3, 2, 4).reshape(b, l, h, dv), state


def hgrn2_mixer(q_l, i_l, zf_l, zb_l, q_c, i_c, zf_c, zb_c, lb, norm_g):
    def heads(t):
        return t.reshape(t.shape[0], t.shape[1], HGRN_HEADS, HGRN_HEAD_DIM)
    bsz = q_l.shape[0]
    qh_l, qh_c = heads(jax.nn.silu(q_l)), heads(jax.nn.silu(q_c))
    ih_l, ih_c = heads(i_l), heads(i_c)
    outs_l, outs_c = [], []
    for d, (z_l, z_c) in enumerate(((zf_l, zf_c), (zb_l, zb_c))):
        g_l, g_c = heads(log_forget(z_l, lb[d])), heads(log_forget(z_c, lb[d]))
        seq_c = (qh_c, -jnp.expm1(g_c), ih_c, g_c)
        seq_l = (qh_l, -jnp.expm1(g_l), ih_l, g_l)
        if d == 1:
            seq_c = tuple(jnp.flip(t, 1) for t in seq_c)
            seq_l = tuple(jnp.flip(t, 1) for t in seq_l)
        s_zero = jnp.zeros((bsz, HGRN_HEADS, HGRN_HEAD_DIM, HGRN_HEAD_DIM), jnp.float32)
        o_c, s_c = gla_chunk_scan(*seq_c, s_zero)
        o_l, _ = gla_chunk_scan(*seq_l, s_c)
        if d == 1:
            o_c, o_l = jnp.flip(o_c, 1), jnp.flip(o_l, 1)
        outs_c.append(o_c)
        outs_l.append(o_l)

    def finish(o, like):
        return rms_norm(o, norm_g).reshape(o.shape[0], o.shape[1], GROUP_W).astype(like.dtype)
    return finish(outs_l[0] + outs_l[1], q_l), finish(outs_c[0] + outs_c[1], q_c)


def conformer_conv(v, g, w, b, ln_g, ln_b):
    y = v * jax.nn.sigmoid(g)
    y = depthwise_conv(y, w, b, CONV_K // 2, CONV_K // 2)
    return jax.nn.silu(layer_norm(y, ln_g, ln_b))


def diff_attention_mixer(q_l, k_l, v_l, q_c, k_c, v_c, cos, sin, lam, lam_init, norm_g, with_ctx):
    bsz, l, _ = q_l.shape
    h2 = 2 * DIFF_HEADS

    def sub_heads(t):
        return t.reshape(t.shape[0], t.shape[1], h2, DIFF_HEAD_DIM)

    def val_heads(t):
        return t.reshape(t.shape[0], t.shape[1], DIFF_HEADS, 2 * DIFF_HEAD_DIM)

    scale = DIFF_HEAD_DIM ** -0.5

    def attend(q, keys, vals):
        s = jnp.einsum('bqhd,bkhd->bhqk', q, keys).astype(jnp.float32) * scale
        p = jax.nn.softmax(s, axis=-1)
        p = p.reshape(p.shape[0], DIFF_HEADS, 2, p.shape[2], p.shape[3])
        a = (p[:, :, 0] - lam * p[:, :, 1]).astype(vals.dtype)
        return jnp.einsum('bhqk,bkhe->bqhe', a, vals)

    def finish(o):
        return (rms_norm(o, norm_g) * (1.0 - lam_init)).reshape(o.shape[0], o.shape[1], GROUP_W)

    kh_c, vh_c = sub_heads(k_c), val_heads(v_c)
    keys = jnp.concatenate([kh_c, apply_rope(sub_heads(k_l), cos, sin)], axis=1)
    vals = jnp.concatenate([vh_c, val_heads(v_l)], axis=1)
    n_blk = l // Q_BLOCK
    q_rot = apply_rope(sub_heads(q_l), cos, sin)
    q_blocks = q_rot.reshape(bsz, n_blk, Q_BLOCK, h2, DIFF_HEAD_DIM).transpose(1, 0, 2, 3, 4)
    o_l = lax.map(lambda qb: attend(qb, keys, vals), q_blocks)
    o_l = o_l.transpose(1, 0, 2, 3, 4).reshape(bsz, l, DIFF_HEADS, 2 * DIFF_HEAD_DIM)
    y_l = finish(o_l)
    y_c = finish(attend(sub_heads(q_c), kh_c, vh_c)) if with_ctx else None
    return y_l, y_c


def hybrid_layer(x, xc, c, c_ctx, layer, with_ctx, lb, cos, sin,
                 w_mod, b_mod, g_pre, g_post, w_in, w_out,
                 lru_conv_w, lru_conv_b, lru_w_r, lru_b_r, lru_w_i, lru_b_i, lru_lambda,
                 hgrn_norm_g, conf_conv_w, conf_conv_b, conf_ln_g, conf_ln_b,
                 lam_q1, lam_k1, lam_q2, lam_k2, diff_norm_g):
    mod = jax.nn.silu(c) @ w_mod + b_mod
    mod_c = jax.nn.silu(c_ctx) @ w_mod + b_mod
    shift, scale, gate = jnp.split(mod, 3, axis=-1)
    shift_c, scale_c, gate_c = jnp.split(mod_c, 3, axis=-1)
    h = rms_norm(x, g_pre) * (1.0 + scale[:, None]) + shift[:, None]
    hc = rms_norm(xc, g_pre) * (1.0 + scale_c) + shift_c
    u = jnp.split(h @ w_in, N_IN_SLICES, axis=-1)
    uc = jnp.split(hc @ w_in, N_IN_SLICES, axis=-1)

    ya, ya_c = rglru_mixer(u[0], uc[0], lru_conv_w, lru_conv_b, lru_w_r, lru_b_r, lru_w_i, lru_b_i, lru_lambda)
    yb, yb_c = hgrn2_mixer(u[2], u[3], u[4], u[5], uc[2], uc[3], uc[4], uc[5], lb, hgrn_norm_g)
    yc = conformer_conv(u[7], u[8], conf_conv_w, conf_conv_b, conf_ln_g, conf_ln_b)
    lam_init = 0.8 - 0.6 * math.exp(-0.3 * layer)
    lam = (jnp.exp(jnp.sum(lam_q1.astype(jnp.float32) * lam_k1.astype(jnp.float32)))
           - jnp.exp(jnp.sum(lam_q2.astype(jnp.float32) * lam_k2.astype(jnp.float32))) + lam_init)
    yd, yd_c = diff_attention_mixer(u[10], u[11], u[12], uc[10], uc[11], uc[12], cos, sin,
                                    lam, lam_init, diff_norm_g, with_ctx)

    def merge(ys, gs):
        return jnp.concatenate([y * jax.nn.silu(g) for y, g in zip(ys, gs)], axis=-1) @ w_out

    x = x + gate[:, None] * rms_norm(merge((ya, yb, yc, yd), (u[1], u[6], u[9], u[13])), g_post)
    if with_ctx:
        yc_c = conformer_conv(uc[7], uc[8], conf_conv_w, conf_conv_b, conf_ln_g, conf_ln_b)
        xc = xc + gate_c * rms_norm(merge((ya_c, yb_c, yc_c, yd_c), (uc[1], uc[6], uc[9], uc[13])), g_post)
    return x, xc


def setup_inputs(seed: int = 0) -> dict:
    key = jax.random.key(seed)
    ks = jax.random.split(key, 28)
    f32 = jnp.float32
    D = D_MODEL

    def nrm(k, shape, s):
        return jax.random.normal(k, shape, f32) * s

    a_pow = jax.random.uniform(ks[16], (DEPTH, 2, GROUP_W), f32, 0.9, 0.999)
    sig = a_pow ** (1.0 / LRU_C)
    return {
        'x': nrm(ks[0], (BATCH, SEQ, D), 1.0),
        'c': nrm(ks[1], (BATCH, D), 1.0),
        'ctx': nrm(ks[2], (BATCH, CTX_LEN, D), 1.0),
        'c_ctx': nrm(ks[3], (D,), 1.0),
        'w_mod': nrm(ks[4], (DEPTH, D, 3 * D), 0.5 * D ** -0.5),
        'b_mod': nrm(ks[5], (DEPTH, 3 * D), 0.02),
        'g_pre': 1.0 + nrm(ks[6], (DEPTH, D), 0.02),
        'g_post': 1.0 + nrm(ks[7], (DEPTH, D), 0.02),
        'w_in': nrm(ks[8], (DEPTH, D, IN_W), D ** -0.5),
        'w_out': nrm(ks[9], (DEPTH, MIX_W, D), MIX_W ** -0.5),
        'lru_conv_w': nrm(ks[10], (DEPTH, LRU_CONV, GROUP_W), LRU_CONV ** -0.5),
        'lru_conv_b': nrm(ks[11], (DEPTH, GROUP_W), 0.02),
        'lru_w_r': nrm(ks[12], (DEPTH, 2, LRU_HEADS, LRU_BLOCK, LRU_BLOCK), LRU_BLOCK ** -0.5),
        'lru_b_r': nrm(ks[13], (DEPTH, 2, GROUP_W), 0.02),
        'lru_w_i': nrm(ks[14], (DEPTH, 2, LRU_HEADS, LRU_BLOCK, LRU_BLOCK), LRU_BLOCK ** -0.5),
        'lru_b_i': nrm(ks[15], (DEPTH, 2, GROUP_W), 0.02),
        'lru_lambda': jnp.log(sig) - jnp.log1p(-sig),
        'hgrn_lb': nrm(ks[17], (DEPTH, 2, GROUP_W), 1.0),
        'hgrn_norm_g': 1.0 + nrm(ks[18], (DEPTH, HGRN_HEAD_DIM), 0.02),
        'conf_conv_w': nrm(ks[19], (DEPTH, CONV_K, GROUP_W), CONV_K ** -0.5),
        'conf_conv_b': nrm(ks[20], (DEPTH, GROUP_W), 0.02),
        'conf_ln_g': 1.0 + nrm(ks[21], (DEPTH, GROUP_W), 0.02),
        'conf_ln_b': nrm(ks[22], (DEPTH, GROUP_W), 0.02),
        'diff_lam_q1': nrm(ks[23], (DEPTH, DIFF_HEAD_DIM), 0.1),
        'diff_lam_k1': nrm(ks[24], (DEPTH, DIFF_HEAD_DIM), 0.1),
        'diff_lam_q2': nrm(ks[25], (DEPTH, DIFF_HEAD_DIM), 0.1),
        'diff_lam_k2': nrm(ks[26], (DEPTH, DIFF_HEAD_DIM), 0.1),
        'diff_norm_g': 1.0 + nrm(ks[27], (DEPTH, 2 * DIFF_HEAD_DIM), 0.02),
    }


def reference(x, c, ctx, c_ctx, w_mod, b_mod, g_pre, g_post, w_in, w_out,
              lru_conv_w, lru_conv_b, lru_w_r, lru_b_r, lru_w_i, lru_b_i, lru_lambda,
              hgrn_lb, hgrn_norm_g, conf_conv_w, conf_conv_b, conf_ln_g, conf_ln_b,
              diff_lam_q1, diff_lam_k1, diff_lam_q2, diff_lam_k2, diff_norm_g):
    cos, sin = axial_rope_tables(x.shape[1])
    p = jax.nn.softmax(hgrn_lb.astype(jnp.float32), axis=0)
    lower_bounds = jnp.cumsum(p, axis=0) - p[0]
    xc = ctx
    for layer in range(DEPTH):
        x, xc = hybrid_layer(
            x, xc, c, c_ctx, layer, layer < DEPTH - 1, lower_bounds[layer], cos, sin,
            w_mod[layer], b_mod[layer], g_pre[layer], g_post[layer], w_in[layer], w_out[layer],
            lru_conv_w[layer], lru_conv_b[layer], lru_w_r[layer], lru_b_r[layer],
            lru_w_i[layer], lru_b_i[layer], lru_lambda[layer],
            hgrn_norm_g[layer], conf_conv_w[layer], conf_conv_b[layer], conf_ln_g[layer], conf_ln_b[layer],
            diff_lam_q1[layer], diff_lam_k1[layer], diff_lam_q2[layer], diff_lam_k2[layer], diff_norm_g[layer])
    return x
```

```python
import functools
import math

import jax
import jax.numpy as jnp
from jax import lax
from jax.experimental import pallas as pl
from jax.experimental.pallas import tpu as pltpu

F32 = jnp.float32
BF16 = jnp.bfloat16

D_MODEL = 1024
CTX_LEN = 256
GRID_W = 64
GROUP_W = 256
N_IN_SLICES = 14
IN_W = GROUP_W * N_IN_SLICES
LRU_CONV = 4
LRU_C = 8.0
HEAD_W = 64
SUB_HEAD_W = 32
N_PAIRS = GROUP_W // HEAD_W
LB_FLOOR = 1e-30
CONV_K = 31
ROPE_THETA = 10000.0
RMS_EPS = 1e-6
LN_EPS = 1e-5

SUBLANES = 8
TOK_TILE = 256
TIME_CHUNK = 256
CONF_HALO = 16
LRU_HALO_L = 2
LRU_HALO_R = 1
HGRN_CHUNK = 64
HGRN_EXACT = 8
MOD_N_TILE = 1024


def _sigmoid(x):
    return 1.0 / (1.0 + jnp.exp(-x))


def _silu(x):
    return x * _sigmoid(x)


def _softplus(x):
    return jnp.maximum(x, 0.0) + jnp.log1p(jnp.exp(-jnp.abs(x)))


def _div_pow2(x, n):
    assert n & (n - 1) == 0
    return x >> (n.bit_length() - 1)


def _head_mean_matrix(width, dtype):
    r = _div_pow2(lax.broadcasted_iota(jnp.int32, (GROUP_W, GROUP_W), 0), width)
    c = _div_pow2(lax.broadcasted_iota(jnp.int32, (GROUP_W, GROUP_W), 1), width)
    return (r == c).astype(dtype)


def _head_rms_norm(o, gain):
    ms = jnp.dot(o * o, _head_mean_matrix(HEAD_W, F32), precision=lax.Precision.HIGHEST,
                 preferred_element_type=F32) * (1.0 / HEAD_W)
    return o * lax.rsqrt(ms + RMS_EPS) * gain


def _mod_kernel(cc_ref, w_ref, b_ref, o_ref):
    s = _silu(cc_ref[...])
    o_ref[0] = jnp.dot(s, w_ref[0], precision=lax.Precision.HIGHEST,
                       preferred_element_type=F32) + b_ref[0]


def _modulation(cc, w_mod, b_mod):
    depth, d, n = w_mod.shape
    rows = cc.shape[0]
    return pl.pallas_call(
        _mod_kernel,
        grid=(depth, n // MOD_N_TILE),
        in_specs=[
            pl.BlockSpec((rows, d), lambda l, j: (0, 0)),
            pl.BlockSpec((1, d, MOD_N_TILE), lambda l, j: (l, 0, j)),
            pl.BlockSpec((1, 1, MOD_N_TILE), lambda l, j: (l, 0, j)),
        ],
        out_specs=pl.BlockSpec((1, rows, MOD_N_TILE), lambda l, j: (l, 0, j)),
        out_shape=jax.ShapeDtypeStruct((depth, rows, n), F32),
        name="modulation",
    )(cc, w_mod, b_mod)


def _lb_kernel(x_ref, o_ref):
    x = x_ref[...]
    e = jnp.exp(x - jnp.max(x, axis=0, keepdims=True))
    p = e / jnp.sum(e, axis=0, keepdims=True)
    acc = jnp.zeros_like(p[0:1])
    for l in range(x.shape[0]):
        if l > 0:
            acc = acc + p[l:l + 1]
        o_ref[l:l + 1, :] = acc


def _lower_bounds(lb_raw):
    return pl.pallas_call(
        _lb_kernel, out_shape=jax.ShapeDtypeStruct(lb_raw.shape, F32), name="hgrn_lower_bounds",
    )(lb_raw)


def _inproj_kernel(x_ref, mod_ref, g_ref, w_ref, u_ref, *, ctx_row):
    b = pl.program_id(0)
    t = pl.program_id(1)
    row = jnp.where(t == 0, ctx_row, b)
    mod = mod_ref[pl.ds(row, 1), :]
    shift = mod[:, :D_MODEL]
    scale = mod[:, D_MODEL:2 * D_MODEL]
    x = x_ref[0]
    y = x * lax.rsqrt(jnp.mean(x * x, axis=-1, keepdims=True) + RMS_EPS) * g_ref[...]
    h = y * (1.0 + scale) + shift
    u_ref[0] = jnp.dot(h.astype(BF16), w_ref[...], preferred_element_type=F32)


def _inproj(xs, mod, g_pre, w_in, ctx_row):
    bsz, ltot, d = xs.shape
    return pl.pallas_call(
        functools.partial(_inproj_kernel, ctx_row=ctx_row),
        grid=(bsz, ltot // TOK_TILE),
        in_specs=[
            pl.BlockSpec((1, TOK_TILE, d), lambda b, t: (b, t, 0)),
            pl.BlockSpec(mod.shape, lambda b, t: (0, 0)),
            pl.BlockSpec((1, d), lambda b, t: (0, 0)),
            pl.BlockSpec(w_in.shape, lambda b, t: (0, 0)),
        ],
        out_specs=pl.BlockSpec((1, TOK_TILE, IN_W), lambda b, t: (b, t, 0)),
        out_shape=jax.ShapeDtypeStruct((bsz, ltot, IN_W), F32),
        name="in_projection",
    )(xs, mod, g_pre, w_in)


def _lru_kernel(xm_ref, xp_ref, xn_ref, cw_ref, cb_ref, wg_ref, bg_ref, lam_ref, o_ref,
                a_scr, u_scr, h_scr, *, n_chunks):
    d = pl.program_id(0)
    i = pl.program_id(1)
    c = _lru_chunk(d, i, n_chunks)
    tc, bn, ch = xm_ref.shape

    @pl.when(i == 0)
    def _():
        h_scr[...] = jnp.zeros_like(h_scr)

    left_ok = c >= 2
    right_ok = jnp.logical_and(c >= 1, c <= n_chunks - 2)
    xp = jnp.where(left_ok, xp_ref[...], 0.0)
    xn = jnp.where(right_ok, xn_ref[...], 0.0)
    xe = jnp.concatenate([xp, xm_ref[...], xn], axis=0)
    xc = jnp.zeros((tc, bn, ch), F32) + cb_ref[...].reshape(1, 1, ch)
    for k in range(LRU_CONV):
        xc = xc + cw_ref[k:k + 1, :].reshape(1, 1, ch) * xe[k:k + tc]

    x2 = xc.reshape(tc * bn, ch)
    gates = jnp.dot(x2.astype(BF16), wg_ref[0], preferred_element_type=F32) + bg_ref[0]
    r = _sigmoid(gates[:, :ch])
    ig = _sigmoid(gates[:, ch:])
    log_a = -LRU_C * r * _softplus(-lam_ref[0])
    a = jnp.exp(log_a)
    uu = jnp.sqrt(jnp.maximum(1.0 - a * a, 0.0)) * (ig * x2)
    a_scr[...] = a.reshape(tc, bn, ch)
    u_scr[...] = uu.reshape(tc, bn, ch)

    def step(j, h):
        t = jnp.where(d == 0, j, tc - 1 - j)
        h = a_scr[t] * h + u_scr[t]
        o_ref[0, t] = h
        return h

    h_scr[...] = lax.fori_loop(0, tc, step, h_scr[...], unroll=SUBLANES)


def _lru_chunk(d, i, n_chunks):
    return jnp.where(d == 0, i, jnp.where(i == 0, 0, n_chunks - i))


def _lru(u0t, conv_w, conv_b, wg, bg, lam):
    ltot, bn, ch = u0t.shape
    n_chunks = ltot // TIME_CHUNK
    main = lambda d, i: (_lru_chunk(d, i, n_chunks), 0, 0)
    prev = lambda d, i: (jnp.maximum(_lru_chunk(d, i, n_chunks) * (TIME_CHUNK // LRU_HALO_L) - 1, 0), 0, 0)
    nxt = lambda d, i: (jnp.minimum((_lru_chunk(d, i, n_chunks) + 1) * TIME_CHUNK, ltot - 1), 0, 0)
    return pl.pallas_call(
        functools.partial(_lru_kernel, n_chunks=n_chunks),
        grid=(2, n_chunks),
        in_specs=[
            pl.BlockSpec((TIME_CHUNK, bn, ch), main),
            pl.BlockSpec((LRU_HALO_L, bn, ch), prev),
            pl.BlockSpec((LRU_HALO_R, bn, ch), nxt),
            pl.BlockSpec(conv_w.shape, lambda d, i: (0, 0)),
            pl.BlockSpec(conv_b.shape, lambda d, i: (0, 0)),
            pl.BlockSpec((1,) + wg.shape[1:], lambda d, i: (d, 0, 0)),
            pl.BlockSpec((1,) + bg.shape[1:], lambda d, i: (d, 0, 0)),
            pl.BlockSpec((1,) + lam.shape[1:], lambda d, i: (d, 0, 0)),
        ],
        out_specs=pl.BlockSpec((1, TIME_CHUNK, bn, ch),
                               lambda d, i: (d, _lru_chunk(d, i, n_chunks), 0, 0)),
        out_shape=jax.ShapeDtypeStruct((2, ltot, bn, ch), F32),
        scratch_shapes=[
            pltpu.VMEM((TIME_CHUNK, bn, ch), F32),
            pltpu.VMEM((TIME_CHUNK, bn, ch), F32),
            pltpu.VMEM((bn, ch), F32),
        ],
        name="rglru",
    )(u0t, u0t, u0t, conv_w, conv_b, wg, bg, lam)


def _conf_kernel(xm_ref, xp_ref, xn_ref, cw_ref, cb_ref, lg_ref, lb_ref, o_ref, y_scr, *, n_chunks):
    c = pl.program_id(0)
    tc, bn, _ = xm_ref.shape
    ch = GROUP_W

    def glu(x):
        return x[..., :ch] * _sigmoid(x[..., ch:])

    left_ok = c >= 2
    right_ok = jnp.logical_and(c >= 1, c <= n_chunks - 2)
    y_scr[0:CONF_HALO] = jnp.where(left_ok, glu(xp_ref[...]), 0.0)
    y_scr[CONF_HALO:CONF_HALO + tc] = glu(xm_ref[...])
    y_scr[CONF_HALO + tc:] = jnp.where(right_ok, glu(xn_ref[...]), 0.0)

    acc = jnp.zeros((tc, bn, ch), F32) + cb_ref[...].reshape(1, 1, ch)
    first = CONF_HALO - CONV_K // 2
    for k in range(CONV_K):
        acc = acc + cw_ref[k:k + 1, :].reshape(1, 1, ch) * y_scr[first + k:first + k + tc]

    z = acc.reshape(tc * bn, ch)
    mu = jnp.mean(z, axis=-1, keepdims=True)
    zc = z - mu
    var = jnp.mean(zc * zc, axis=-1, keepdims=True)
    y = zc * lax.rsqrt(var + LN_EPS) * lg_ref[...] + lb_ref[...]
    o_ref[...] = _silu(y).reshape(tc, bn, ch)


def _conformer(u78t, conv_w, conv_b, ln_g, ln_b):
    ltot, bn, ch2 = u78t.shape
    n_chunks = ltot // TIME_CHUNK
    per = TIME_CHUNK // CONF_HALO
    return pl.pallas_call(
        functools.partial(_conf_kernel, n_chunks=n_chunks),
        grid=(n_chunks,),
        in_specs=[
            pl.BlockSpec((TIME_CHUNK, bn, ch2), lambda c: (c, 0, 0)),
            pl.BlockSpec((CONF_HALO, bn, ch2), lambda c: (jnp.maximum(c * per - 1, 0), 0, 0)),
            pl.BlockSpec((CONF_HALO, bn, ch2), lambda c: (jnp.minimum((c + 1) * per, n_chunks * per - 1), 0, 0)),
            pl.BlockSpec(conv_w.shape, lambda c: (0, 0)),
            pl.BlockSpec(conv_b.shape, lambda c: (0, 0)),
            pl.BlockSpec(ln_g.shape, lambda c: (0, 0)),
            pl.BlockSpec(ln_b.shape, lambda c: (0, 0)),
        ],
        out_specs=pl.BlockSpec((TIME_CHUNK, bn, GROUP_W), lambda c: (c, 0, 0)),
        out_shape=jax.ShapeDtypeStruct((ltot, bn, GROUP_W), F32),
        scratch_shapes=[pltpu.VMEM((TIME_CHUNK + 2 * CONF_HALO, bn, GROUP_W), F32)],
        name="conformer_conv",
    )(u78t, u78t, u78t, conv_w, conv_b, ln_g, ln_b)


def _hgrn_kernel(x_ref, lb_ref, o_ref, st_ref):
    i = pl.program_id(1)
    bn = x_ref.shape[1]
    ch = GROUP_W
    n = bn * HGRN_CHUNK

    @pl.when(i == 0)
    def _():
        st_ref[...] = jnp.zeros_like(st_ref)

    xs = x_ref[0].reshape(n, 3 * ch)
    q = _silu(xs[:, :ch])
    v = xs[:, ch:2 * ch]
    z = xs[:, 2 * ch:]
    lb = lb_ref[0]
    log_lb = jnp.log(jnp.maximum(lb, LB_FLOOR))
    log_rest = jnp.log1p(-lb) - _softplus(-z)
    g = jnp.maximum(log_lb, log_rest) + jnp.log1p(jnp.exp(-jnp.abs(log_lb - log_rest)))
    k = 1.0 - jnp.exp(g)

    row = lax.broadcasted_iota(jnp.int32, (n, ch), 0)
    in_chunk = row & (HGRN_CHUNK - 1)
    gc = g
    sft = 1
    while sft < HGRN_CHUNK:
        gc = gc + jnp.where(in_chunk >= sft, pltpu.roll(gc, sft, 0), 0.0)
        sft *= 2

    gc3 = gc.reshape(bn, HGRN_CHUNK, ch)
    g_end = gc3[:, HGRN_CHUNK - 1:HGRN_CHUNK, :]
    q_state = (q * jnp.exp(gc)).astype(BF16)
    k_state = (k.reshape(bn, HGRN_CHUNK, ch) * jnp.exp(g_end - gc3)).reshape(n, ch).astype(BF16)
    v16 = v.astype(BF16)

    head_sum = _head_mean_matrix(HEAD_W, BF16)
    in_blk = row & (HGRN_EXACT - 1)
    out = jnp.dot((q * k).astype(BF16), head_sum, preferred_element_type=F32) * v
    for r in range(1, HGRN_EXACT):
        decay = jnp.exp(jnp.minimum(gc - pltpu.roll(gc, r, 0), 0.0))
        w = jnp.where(in_blk >= r, q * pltpu.roll(k, r, 0) * decay, 0.0)
        out = out + jnp.dot(w.astype(BF16), head_sum, preferred_element_type=F32) * pltpu.roll(v, r, 0)

    lane_head = _div_pow2(lax.broadcasted_iota(jnp.int32, (HGRN_CHUNK, ch), 1), HEAD_W)
    qi =lax.broadcasted_iota(jnp.int32, (HGRN_CHUNK, ch), 0)
    kj = lax.broadcasted_iota(jnp.int32, (HGRN_CHUNK, ch), 1) & (HEAD_W - 1)
    levels = []
    s = HGRN_CHUNK // 2
    while s >= HGRN_EXACT:
        g4 = gc.reshape(n // (2 * s), 2 * s, ch)
        anchor = jnp.broadcast_to(g4[:, s - 1:s, :], g4.shape).reshape(n, ch)
        upper = (row & (2 * s - 1)) >= s
        qs = jnp.where(upper, q * jnp.exp(jnp.minimum(gc - anchor, 0.0)), 0.0).astype(BF16)
        ks = jnp.where(upper, 0.0, k * jnp.exp(jnp.minimum(anchor - gc, 0.0))).astype(BF16)
        same_parent = _div_pow2(qi, 2 * s) == _div_pow2(kj, 2 * s)
        levels.append((qs, ks, same_parent))
        s //= 2

    def head_stack(t):
        return jnp.concatenate([jnp.where(lane_head == h, t, jnp.zeros_like(t)) for h in range(ch // HEAD_W)], axis=0)

    nt = (((1,), (1,)), ((), ()))
    tn = (((0,), (0,)), ((), ()))
    bd = _head_mean_matrix(HEAD_W, F32) > 0.5
    outs = []
    for b in range(bn):
        sl = slice(b * HGRN_CHUNK, (b + 1) * HGRN_CHUNK)
        scores = jnp.zeros((HGRN_CHUNK, ch), F32)
        for qs, ks, same_parent in levels:
            sc = lax.dot_general(qs[sl], head_stack(ks[sl]), nt, preferred_element_type=F32)
            scores = scores + jnp.where(same_parent, sc, 0.0)
        o_b = jnp.dot(scores.astype(BF16), head_stack(v16[sl]), preferred_element_type=F32)
        st = st_ref[b]
        o_b = o_b + lax.dot_general(q_state[sl], st.astype(BF16), nt, preferred_element_type=F32)
        upd = lax.dot_general(v16[sl], k_state[sl], tn, preferred_element_type=F32)
        st_ref[b] = st * jnp.exp(g_end[b]) + jnp.where(bd, upd, 0.0)
        outs.append(o_b)
    o_ref[0] = (out + jnp.concatenate(outs, axis=0)).reshape(bn, HGRN_CHUNK, ch)


def _hgrn(hg, lb):
    _, bn, ltot, w = hg.shape
    return pl.pallas_call(
        _hgrn_kernel,
        grid=(2, ltot // HGRN_CHUNK),
        in_specs=[
            pl.BlockSpec((1, bn, HGRN_CHUNK, w), lambda d, i: (d, 0, i, 0)),
            pl.BlockSpec((1, 1, GROUP_W), lambda d, i: (d, 0, 0)),
        ],
        out_specs=pl.BlockSpec((1, bn, HGRN_CHUNK, GROUP_W), lambda d, i: (d, 0, i, 0)),
        out_shape=jax.ShapeDtypeStruct((2, bn, ltot, GROUP_W), F32),
        scratch_shapes=[pltpu.VMEM((bn, GROUP_W, GROUP_W), F32)],
        name="hgrn2",
    )(hg, lb)


def _rope_kernel(q_ref, k_ref, v_ref, cos_ref, sin_ref, qo_ref, kt_ref, vo_ref):
    cos = cos_ref[...]
    sin = sin_ref[...]
    lane = lax.broadcasted_iota(jnp.int32, cos.shape, 1)
    first_half = (lane & (SUB_HEAD_W - 1)) < SUB_HEAD_W // 2

    def rope(x):
        partner = jnp.where(first_half, pltpu.roll(x, GROUP_W - SUB_HEAD_W // 2, 1),
                            pltpu.roll(x, SUB_HEAD_W // 2, 1))
        return x * cos + partner * sin

    qo_ref[0] = (rope(q_ref[0]) * (SUB_HEAD_W ** -0.5)).astype(BF16)
    kt_ref[0] = rope(k_ref[0]).T.astype(BF16)
    vo_ref[0] = v_ref[0].astype(BF16)


def _rope(u, cos_t, sin_t):
    bsz, ltot, _ = u.shape
    col = lambda j: (lambda b, t: (b, t, j))
    tok = pl.BlockSpec((1, TOK_TILE, GROUP_W), lambda b, t: (b, t, 0))
    tab = pl.BlockSpec((TOK_TILE, GROUP_W), lambda b, t: (t, 0))
    return pl.pallas_call(
        _rope_kernel,
        grid=(bsz, ltot // TOK_TILE),
        in_specs=[
            pl.BlockSpec((1, TOK_TILE, GROUP_W), col(10)),
            pl.BlockSpec((1, TOK_TILE, GROUP_W), col(11)),
            pl.BlockSpec((1, TOK_TILE, GROUP_W), col(12)),
            tab, tab,
        ],
        out_specs=[tok, pl.BlockSpec((1, GROUP_W, TOK_TILE), lambda b, t: (b, 0, t)), tok],
        out_shape=[
            jax.ShapeDtypeStruct((bsz, ltot, GROUP_W), BF16),
            jax.ShapeDtypeStruct((bsz, GROUP_W, ltot), BF16),
            jax.ShapeDtypeStruct((bsz, ltot, GROUP_W), BF16),
        ],
        name="rope_qkv",
    )(u, u, u, cos_t, sin_t)


def _attn_kernel(lamp_ref, g_ref, q_ref, kt_ref, v_ref, o_ref, vm_ref, *, lam_init, t_off):
    t = pl.program_id(1) + t_off
    ltot = v_ref.shape[1]
    tq = q_ref.shape[1]

    @pl.when(pl.program_id(1) == 0)
    def _():
        v = v_ref[0]
        lane_head = _div_pow2(lax.broadcasted_iota(jnp.int32, v.shape, 1), HEAD_W)
        for h in range(N_PAIRS):
            vm_ref[h] = jnp.where(lane_head == h, v, jnp.zeros_like(v))

    lp = lamp_ref[...]
    lam = (jnp.exp(jnp.sum(lp[0:1] * lp[1:2], axis=-1, keepdims=True))
           - jnp.exp(jnp.sum(lp[2:3] * lp[3:4], axis=-1, keepdims=True)) + lam_init)

    def attend(nk):
        q = q_ref[0]
        sub = _div_pow2(lax.broadcasted_iota(jnp.int32, q.shape, 1), SUB_HEAD_W)
        o = jnp.zeros((tq, GROUP_W), F32)
        for j in range(N_PAIRS):
            lhs = jnp.concatenate([jnp.where(sub == 2 * j, q, jnp.zeros_like(q)),
                                   jnp.where(sub == 2 * j + 1, q, jnp.zeros_like(q))], axis=0)
            s = jnp.dot(lhs, kt_ref[0, :, :nk], preferred_element_type=F32)
            p = jnp.exp(s - jnp.max(s, axis=-1, keepdims=True))
            inv = 1.0 / jnp.sum(p, axis=-1, keepdims=True)
            a = p[:tq] * inv[:tq] - p[tq:] * (lam * inv[tq:])
            o = o + jnp.dot(a.astype(BF16), vm_ref[j, :nk, :], preferred_element_type=F32)
        o_ref[0] = _head_rms_norm(o, g_ref[...]) * (1.0 - lam_init)

    @pl.when(t == 0)
    def _():
        attend(CTX_LEN)

    @pl.when(t > 0)
    def _():
        attend(ltot)


def _attention(lam_params, norm_g, q_rot, k_t, v16, lam_init, with_ctx):
    bsz, ltot, _ = q_rot.shape
    t_off = 0 if with_ctx else CTX_LEN // TOK_TILE
    n_t = ltot // TOK_TILE - t_off
    return pl.pallas_call(
        functools.partial(_attn_kernel, lam_init=lam_init, t_off=t_off),
        grid=(bsz, n_t),
        in_specs=[
            pl.BlockSpec(lam_params.shape, lambda b, t: (0, 0)),
            pl.BlockSpec(norm_g.shape, lambda b, t: (0, 0)),
            pl.BlockSpec((1, TOK_TILE, GROUP_W), lambda b, t: (b, t + t_off, 0)),
            pl.BlockSpec((1, GROUP_W, ltot), lambda b, t: (b, 0, 0)),
            pl.BlockSpec((1, ltot, GROUP_W), lambda b, t: (b, 0, 0)),
        ],
        out_specs=pl.BlockSpec((1, TOK_TILE, GROUP_W), lambda b, t: (b, t, 0)),
        out_shape=jax.ShapeDtypeStruct((bsz, n_t * TOK_TILE, GROUP_W), F32),
        scratch_shapes=[pltpu.VMEM((N_PAIRS, ltot, GROUP_W), BF16)],
        name="diff_attention",
    )(lam_params, norm_g, q_rot, k_t, v16)


def _outproj_kernel(x_ref, mod_ref, gpost_ref, hgain_ref, w_ref, ga_ref, gb_ref, gc_ref, gd_ref,
                    ya_ref, yb_ref, yc_ref, yd_ref, o_ref, *, ctx_row, t_off):
    b = pl.program_id(0)
    t = pl.program_id(1) + t_off
    row = jnp.where(t == 0, ctx_row, b)
    gate = mod_ref[pl.ds(row, 1), :][:, 2 * D_MODEL:]
    ya = ya_ref[0, 0] + ya_ref[1, 0]
    yb = _head_rms_norm(yb_ref[0, 0] + yb_ref[1, 0], hgain_ref[...])
    groups = ((ya, ga_ref), (yb, gb_ref), (yc_ref[0], gc_ref), (yd_ref[0], gd_ref))
    acc = jnp.zeros((x_ref.shape[1], D_MODEL), F32)
    for j, (y, g_ref) in enumerate(groups):
        m = (y * _silu(g_ref[0])).astype(BF16)
        acc = acc + jnp.dot(m, w_ref[j * GROUP_W:(j + 1) * GROUP_W, :], preferred_element_type=F32)
    r = acc * lax.rsqrt(jnp.mean(acc * acc, axis=-1, keepdims=True) + RMS_EPS) * gpost_ref[...]
    o_ref[0] = x_ref[0] + gate * r


def _outproj(xs, mod, g_post, hgrn_gain, w_out, u, ya2, yb2, yc, yd, ctx_row, with_ctx):
    bsz, ltot, d = xs.shape
    t_off = 0 if with_ctx else CTX_LEN // TOK_TILE
    n_t = ltot // TOK_TILE - t_off
    assert yd.shape[1] == n_t * TOK_TILE
    col =lambda j: (lambda b, t: (b, t + t_off, j))
    tok = lambda b, t: (b, t + t_off, 0)
    two = lambda b, t: (0, b, t + t_off, 0)
    grp = pl.BlockSpec((1, TOK_TILE, GROUP_W), tok)
    return pl.pallas_call(
        functools.partial(_outproj_kernel, ctx_row=ctx_row, t_off=t_off),
        grid=(bsz, n_t),
        in_specs=[
            pl.BlockSpec((1, TOK_TILE, d), tok),
            pl.BlockSpec(mod.shape, lambda b, t: (0, 0)),
            pl.BlockSpec((1, d), lambda b, t: (0, 0)),
            pl.BlockSpec((1, GROUP_W), lambda b, t: (0, 0)),
            pl.BlockSpec(w_out.shape, lambda b, t: (0, 0)),
            pl.BlockSpec((1, TOK_TILE, GROUP_W), col(1)),
            pl.BlockSpec((1, TOK_TILE, GROUP_W), col(6)),
            pl.BlockSpec((1, TOK_TILE, GROUP_W), col(9)),
            pl.BlockSpec((1, TOK_TILE, GROUP_W), col(13)),
            pl.BlockSpec((2, 1, TOK_TILE, GROUP_W), two),
            pl.BlockSpec((2, 1, TOK_TILE, GROUP_W), two),
            grp,
            pl.BlockSpec((1, TOK_TILE, GROUP_W), lambda b, t: (b, t, 0)),
        ],
        out_specs=pl.BlockSpec((1, TOK_TILE, d), lambda b, t: (b, t, 0)),
        out_shape=jax.ShapeDtypeStruct((bsz, n_t * TOK_TILE, d), F32),
        name="out_projection",
    )(xs, mod, g_post, hgrn_gain, w_out, u, u, u, u, ya2, yb2, yc, yd)


def _rope_tables(n_latent):
    rows = n_latent // GRID_W
    row = jnp.repeat(jnp.arange(rows, dtype=F32), GRID_W)
    colp = jnp.tile(jnp.arange(GRID_W, dtype=F32), rows)
    n_freq = SUB_HEAD_W // 4
    inv_freq = ROPE_THETA ** (-jnp.arange(n_freq, dtype=F32) / n_freq)
    ang = jnp.concatenate([row[:, None] * inv_freq, colp[:, None] * inv_freq], axis=-1)
    cos, sin = jnp.cos(ang), jnp.sin(ang)
    n_sub = GROUP_W // SUB_HEAD_W
    cos_t = jnp.tile(jnp.concatenate([cos, cos], axis=-1), (1, n_sub))
    sin_t = jnp.tile(jnp.concatenate([-sin, sin], axis=-1), (1, n_sub))
    cos_t = jnp.concatenate([jnp.ones((CTX_LEN, GROUP_W), F32), cos_t], axis=0)
    sin_t = jnp.concatenate([jnp.zeros((CTX_LEN, GROUP_W), F32), sin_t], axis=0)
    return cos_t, sin_t


def _flip_segments(t):
    return jnp.concatenate([jnp.flip(t[:, :CTX_LEN], axis=1), jnp.flip(t[:, CTX_LEN:], axis=1)], axis=1)


def _block_diag(w):
    return jax.scipy.linalg.block_diag(*[w[h] for h in range(w.shape[0])])


def kernel(x, c, ctx, c_ctx, w_mod, b_mod, g_pre, g_post, w_in, w_out, lru_conv_w, lru_conv_b, lru_w_r, lru_b_r, lru_w_i, lru_b_i, lru_lambda, hgrn_lb, hgrn_norm_g, conf_conv_w, conf_conv_b, conf_ln_g, conf_ln_b, diff_lam_q1, diff_lam_k1, diff_lam_q2, diff_lam_k2, diff_norm_g):
    bsz, n_latent, d = x.shape
    depth = w_mod.shape[0]
    assert d == D_MODEL and ctx.shape[1] == CTX_LEN and n_latent % TOK_TILE == 0
    assert CTX_LEN == TOK_TILE == TIME_CHUNK

    ctx_row = bsz
    mod_rows = -(-(bsz + 1) // SUBLANES) * SUBLANES
    cc = jnp.zeros((mod_rows, d), F32).at[:bsz].set(c).at[ctx_row].set(c_ctx)
    mod = _modulation(cc, w_mod, b_mod[:, None, :])
    lbs = _lower_bounds(hgrn_lb.reshape(depth, 2 * GROUP_W)).reshape(depth, 2, 1, GROUP_W)
    cos_t, sin_t = _rope_tables(n_latent)

    xs = jnp.concatenate([ctx, x], axis=1)
    for layer in range(depth):
        with_ctx = layer < depth - 1
        u = _inproj(xs, mod[layer], g_pre[layer][None], w_in[layer].astype(BF16), ctx_row)

        wg = jnp.stack([jnp.concatenate([_block_diag(lru_w_r[layer, dd]), _block_diag(lru_w_i[layer, dd])], axis=1)
                        for dd in range(2)]).astype(BF16)
        bg = jnp.concatenate([lru_b_r[layer], lru_b_i[layer]], axis=-1)[:, None, :]
        h_t = _lru(jnp.transpose(u[:, :, :GROUP_W], (1, 0, 2)), lru_conv_w[layer], lru_conv_b[layer][None],
                   wg, bg, lru_lambda[layer][:, None, :])
        ya2 = jnp.transpose(h_t, (0, 2, 1, 3))

        yc_t = _conformer(jnp.transpose(u[:, :, 7 * GROUP_W:9 * GROUP_W], (1, 0, 2)), conf_conv_w[layer],
                          conf_conv_b[layer][None], conf_ln_g[layer][None], conf_ln_b[layer][None])
        yc = jnp.transpose(yc_t, (1, 0, 2))

        fwd = u[:, :, 2 * GROUP_W:5 * GROUP_W]
        bwd = _flip_segments(jnp.concatenate([u[:, :, 2 * GROUP_W:4 * GROUP_W], u[:, :, 5 * GROUP_W:6 * GROUP_W]], axis=-1))
        o2 = _hgrn(jnp.stack([fwd, bwd]), lbs[layer])
        yb2 = jnp.stack([o2[0], _flip_segments(o2[1])])

        lam_init = 0.8 - 0.6 * math.exp(-0.3 * layer)
        lam_params = jnp.stack([diff_lam_q1[layer], diff_lam_k1[layer], diff_lam_q2[layer], diff_lam_k2[layer]])
        q_rot, k_t, v16 = _rope(u, cos_t, sin_t)
        yd = _attention(lam_params, jnp.tile(diff_norm_g[layer], N_PAIRS)[None], q_rot, k_t, v16, lam_init, with_ctx)

        xs = _outproj(xs, mod[layer], g_post[layer][None], jnp.tile(hgrn_norm_g[layer], N_PAIRS)[None],
                      w_out[layer].astype(BF16), u, ya2, yb2, yc, yd, ctx_row, with_ctx)
    return xs
```

```python
import functools
import math

import jax
import jax.numpy as jnp
from jax import lax
from jax.experimental import pallas as pl
from jax.experimental.pallas import tpu as pltpu

F32 = jnp.float32
BF16 = jnp.bfloat16

D_MODEL = 1024
CTX_LEN = 256
GRID_W = 64
GROUP_W = 256
N_IN_SLICES = 14
IN_W = GROUP_W * N_IN_SLICES
LRU_CONV = 4
LRU_C = 8.0
HEAD_W = 64
SUB_HEAD_W = 32
N_PAIRS = GROUP_W // HEAD_W
LB_FLOOR = 1e-30
CONV_K = 31
ROPE_THETA = 10000.0
RMS_EPS = 1e-6
LN_EPS = 1e-5
LOG2_E = math.log2(math.e)

SUBLANES = 8
TOK_TILE = 256
TIME_CHUNK = 256
CONF_HALO = 16
LRU_HALO_L = 2
LRU_HALO_R = 1
HGRN_CHUNK = 64
HGRN_EXACT = 8
MOD_N_TILE = 1024


def _sigmoid(x):
    return 1.0 / (1.0 + jnp.exp(-x))


def _silu(x):
    return x * _sigmoid(x)


def _softplus(x):
    return jnp.maximum(x, 0.0) + jnp.log1p(jnp.exp(-jnp.abs(x)))


def _div_pow2(x, n):
    assert n & (n - 1) == 0
    return x >> (n.bit_length() - 1)


def _head_mean_matrix(width, dtype):
    r = _div_pow2(lax.broadcasted_iota(jnp.int32, (GROUP_W, GROUP_W), 0), width)
    c = _div_pow2(lax.broadcasted_iota(jnp.int32, (GROUP_W, GROUP_W), 1), width)
    return (r == c).astype(dtype)


def _head_rms_norm(o, gain):
    ms = jnp.dot(o * o, _head_mean_matrix(HEAD_W, F32), precision=lax.Precision.HIGHEST,
                 preferred_element_type=F32) * (1.0 / HEAD_W)
    return o * lax.rsqrt(ms + RMS_EPS) * gain


def _mod_kernel(cc_ref, w_ref, b_ref, o_ref):
    s = _silu(cc_ref[...])
    o_ref[0] = jnp.dot(s, w_ref[0], precision=lax.Precision.HIGHEST,
                       preferred_element_type=F32) + b_ref[0]


def _modulation(cc, w_mod, b_mod):
    depth, d, n = w_mod.shape
    rows = cc.shape[0]
    return pl.pallas_call(
        _mod_kernel,
        grid=(depth, n // MOD_N_TILE),
        in_specs=[
            pl.BlockSpec((rows, d), lambda l, j: (0, 0)),
            pl.BlockSpec((1, d, MOD_N_TILE), lambda l, j: (l, 0, j)),
            pl.BlockSpec((1, 1, MOD_N_TILE), lambda l, j: (l, 0, j)),
        ],
        out_specs=pl.BlockSpec((1, rows, MOD_N_TILE), lambda l, j: (l, 0, j)),
        out_shape=jax.ShapeDtypeStruct((depth, rows, n), F32),
        name="modulation",
    )(cc, w_mod, b_mod)


def _lb_kernel(x_ref, o_ref):
    x = x_ref[...]
    e = jnp.exp(x - jnp.max(x, axis=0, keepdims=True))
    p = e / jnp.sum(e, axis=0, keepdims=True)
    acc = jnp.zeros_like(p[0:1])
    for l in range(x.shape[0]):
        if l > 0:
            acc = acc + p[l:l + 1]
        o_ref[l:l + 1, :] = acc


def _lower_bounds(lb_raw):
    return pl.pallas_call(
        _lb_kernel, out_shape=jax.ShapeDtypeStruct(lb_raw.shape, F32), name="hgrn_lower_bounds",
    )(lb_raw)


def _inproj_kernel(x_ref, mod_ref, g_ref, w_ref, u_ref, *, ctx_row):
    b = pl.program_id(0)
    t = pl.program_id(1)
    row = jnp.where(t == 0, ctx_row, b)
    mod = mod_ref[pl.ds(row, 1), :]
    shift = mod[:, :D_MODEL]
    scale = mod[:, D_MODEL:2 * D_MODEL]
    x = x_ref[0]
    y = x * lax.rsqrt(jnp.mean(x * x, axis=-1, keepdims=True) + RMS_EPS) * g_ref[...]
    h = y * (1.0 + scale) + shift
    u_ref[0] = jnp.dot(h.astype(BF16), w_ref[...], preferred_element_type=F32)


def _inproj(xs, mod, g_pre, w_in, ctx_row):
    bsz, ltot, d = xs.shape
    return pl.pallas_call(
        functools.partial(_inproj_kernel, ctx_row=ctx_row),
        grid=(bsz, ltot // TOK_TILE),
        in_specs=[
            pl.BlockSpec((1, TOK_TILE, d), lambda b, t: (b, t, 0)),
            pl.BlockSpec(mod.shape, lambda b, t: (0, 0)),
            pl.BlockSpec((1, d), lambda b, t: (0, 0)),
            pl.BlockSpec(w_in.shape, lambda b, t: (0, 0)),
        ],
        out_specs=pl.BlockSpec((1, TOK_TILE, IN_W), lambda b, t: (b, t, 0)),
        out_shape=jax.ShapeDtypeStruct((bsz, ltot, IN_W), F32),
        name="in_projection",
    )(xs, mod, g_pre, w_in)


def _lru_kernel(xm_ref, xp_ref, xn_ref, cw_ref, cb_ref, wg_ref, bg_ref, lam_ref, o_ref,
                a_scr, u_scr, h_scr, *, n_chunks):
    d = pl.program_id(0)
    i = pl.program_id(1)
    c = _lru_chunk(d, i, n_chunks)
    tc, bn, ch = xm_ref.shape

    @pl.when(i == 0)
    def _():
        h_scr[...] = jnp.zeros_like(h_scr)

    left_ok = c >= 2
    right_ok = jnp.logical_and(c >= 1, c <= n_chunks - 2)
    xp = jnp.where(left_ok, xp_ref[...], 0.0)
    xn = jnp.where(right_ok, xn_ref[...], 0.0)
    xe = jnp.concatenate([xp, xm_ref[...], xn], axis=0)
    xc = jnp.zeros((tc, bn, ch), F32) + cb_ref[...].reshape(1, 1, ch)
    for k in range(LRU_CONV):
        xc = xc + cw_ref[k:k + 1, :].reshape(1, 1, ch) * xe[k:k + tc]

    x2 = xc.reshape(tc * bn, ch)
    gates = jnp.dot(x2.astype(BF16), wg_ref[0], preferred_element_type=F32) + bg_ref[0]
    r = _sigmoid(gates[:, :ch])
    ig = _sigmoid(gates[:, ch:])
    log_a = -LRU_C * r * _softplus(-lam_ref[0])
    a = jnp.exp(log_a)
    uu = jnp.sqrt(jnp.maximum(1.0 - a * a, 0.0)) * (ig * x2)
    a_scr[...] = a.reshape(tc, bn, ch)
    u_scr[...] = uu.reshape(tc, bn, ch)

    def step(j, h):
        t = jnp.where(d == 0, j, tc - 1 - j)
        h = a_scr[t] * h + u_scr[t]
        o_ref[0, t] = h
        return h

    h_scr[...] = lax.fori_loop(0, tc, step, h_scr[...], unroll=SUBLANES)


def _lru_chunk(d, i, n_chunks):
    return jnp.where(d == 0, i, jnp.where(i == 0, 0, n_chunks - i))


def _lru(u0t, conv_w, conv_b, wg, bg, lam):
    ltot, bn, ch = u0t.shape
    n_chunks = ltot // TIME_CHUNK
    main = lambda d, i: (_lru_chunk(d, i, n_chunks), 0, 0)
    prev = lambda d, i: (jnp.maximum(_lru_chunk(d, i, n_chunks) * (TIME_CHUNK // LRU_HALO_L) - 1, 0), 0, 0)
    nxt = lambda d, i: (jnp.minimum((_lru_chunk(d, i, n_chunks) + 1) * TIME_CHUNK, ltot - 1), 0, 0)
    return pl.pallas_call(
        functools.partial(_lru_kernel, n_chunks=n_chunks),
        grid=(2, n_chunks),
        in_specs=[
            pl.BlockSpec((TIME_CHUNK, bn, ch), main),
            pl.BlockSpec((LRU_HALO_L, bn, ch), prev),
            pl.BlockSpec((LRU_HALO_R, bn, ch), nxt),
            pl.BlockSpec(conv_w.shape, lambda d, i: (0, 0)),
            pl.BlockSpec(conv_b.shape, lambda d, i: (0, 0)),
            pl.BlockSpec((1,) + wg.shape[1:], lambda d, i: (d, 0, 0)),
            pl.BlockSpec((1,) + bg.shape[1:], lambda d, i: (d, 0, 0)),
            pl.BlockSpec((1,) + lam.shape[1:], lambda d, i: (d, 0, 0)),
        ],
        out_specs=pl.BlockSpec((1, TIME_CHUNK, bn, ch),
                               lambda d, i: (d, _lru_chunk(d, i, n_chunks), 0, 0)),
        out_shape=jax.ShapeDtypeStruct((2, ltot, bn, ch), F32),
        scratch_shapes=[
            pltpu.VMEM((TIME_CHUNK, bn, ch), F32),
            pltpu.VMEM((TIME_CHUNK, bn, ch), F32),
            pltpu.VMEM((bn, ch), F32),
        ],
        name="rglru",
    )(u0t, u0t, u0t, conv_w, conv_b, wg, bg, lam)


def _conf_kernel(xm_ref, xp_ref, xn_ref, cw_ref, cb_ref, lg_ref, lb_ref, o_ref, y_scr, *, n_chunks):
    c = pl.program_id(0)
    tc, bn, _ = xm_ref.shape
    ch = GROUP_W

    def glu(x):
        return x[..., :ch] * _sigmoid(x[..., ch:])

    left_ok = c >= 2
    right_ok = jnp.logical_and(c >= 1, c <= n_chunks - 2)
    y_scr[0:CONF_HALO] = jnp.where(left_ok, glu(xp_ref[...]), 0.0)
    y_scr[CONF_HALO:CONF_HALO + tc] = glu(xm_ref[...])
    y_scr[CONF_HALO + tc:] = jnp.where(right_ok, glu(xn_ref[...]), 0.0)

    acc = jnp.zeros((tc, bn, ch), F32) + cb_ref[...].reshape(1, 1, ch)
    first = CONF_HALO - CONV_K // 2
    for k in range(CONV_K):
        acc = acc + cw_ref[k:k + 1, :].reshape(1, 1, ch) * y_scr[first + k:first + k + tc]

    z = acc.reshape(tc * bn, ch)
    mu = jnp.mean(z, axis=-1, keepdims=True)
    zc = z - mu
    var = jnp.mean(zc * zc, axis=-1, keepdims=True)
    y = zc * lax.rsqrt(var + LN_EPS) * lg_ref[...] + lb_ref[...]
    o_ref[...] = _silu(y).reshape(tc, bn, ch)


def _conformer(u78t, conv_w, conv_b, ln_g, ln_b):
    ltot, bn, ch2 = u78t.shape
    n_chunks = ltot // TIME_CHUNK
    per = TIME_CHUNK // CONF_HALO
    return pl.pallas_call(
        functools.partial(_conf_kernel, n_chunks=n_chunks),
        grid=(n_chunks,),
        in_specs=[
            pl.BlockSpec((TIME_CHUNK, bn, ch2), lambda c: (c, 0, 0)),
            pl.BlockSpec((CONF_HALO, bn, ch2), lambda c: (jnp.maximum(c * per - 1, 0), 0, 0)),
            pl.BlockSpec((CONF_HALO, bn, ch2), lambda c: (jnp.minimum((c + 1) * per, n_chunks * per - 1), 0, 0)),
            pl.BlockSpec(conv_w.shape, lambda c: (0, 0)),
            pl.BlockSpec(conv_b.shape, lambda c: (0, 0)),
            pl.BlockSpec(ln_g.shape, lambda c: (0, 0)),
            pl.BlockSpec(ln_b.shape, lambda c: (0, 0)),
        ],
        out_specs=pl.BlockSpec((TIME_CHUNK, bn, GROUP_W), lambda c: (c, 0, 0)),
        out_shape=jax.ShapeDtypeStruct((ltot, bn, GROUP_W), F32),
        scratch_shapes=[pltpu.VMEM((TIME_CHUNK + 2 * CONF_HALO, bn, GROUP_W), F32)],
        name="conformer_conv",
    )(u78t, u78t, u78t, conv_w, conv_b, ln_g, ln_b)


def _hgrn_kernel(q_ref, v_ref, z_ref, lb_ref, o_ref, st_ref, *, reverse):
    i = pl.program_id(0)
    bn = q_ref.shape[0]
    ch = GROUP_W
    n = bn * HGRN_CHUNK

    @pl.when(i == 0)
    def _():
        st_ref[...] = jnp.zeros_like(st_ref)

    def toward_keys(t, r):
        return pltpu.roll(t, (n - r) if reverse else r, 0)

    def toward_keys_in_block(t, r):
        t3 = t.reshape(n // HGRN_EXACT, HGRN_EXACT, ch)
        return pltpu.roll(t3, (HGRN_EXACT - r) if reverse else r, 1).reshape(n, ch)

    def reaches(pos, r, size):
        return (pos < size - r) if reverse else (pos >= r)

    q = _silu(q_ref[...].reshape(n, ch))
    v = v_ref[...].reshape(n, ch)
    z = z_ref[...].reshape(n, ch)
    lb = lb_ref[...]
    f = jnp.maximum(lb, LB_FLOOR) + (1.0 - lb) * _sigmoid(z)
    g = jnp.log(f)
    k = 1.0 - f

    row = lax.broadcasted_iota(jnp.int32, (n, ch), 0)
    in_chunk = row & (HGRN_CHUNK - 1)
    gc = g
    sft = 1
    while sft < HGRN_CHUNK:
        gc = gc + jnp.where(reaches(in_chunk, sft, HGRN_CHUNK), toward_keys(gc, sft), 0.0)
        sft *= 2

    gc3 = gc.reshape(bn, HGRN_CHUNK, ch)
    last = 0 if reverse else HGRN_CHUNK - 1
    g_end = gc3[:, last:last + 1, :]
    q_state = (q * jnp.exp(gc)).astype(BF16)
    k_state = (k.reshape(bn, HGRN_CHUNK, ch) * jnp.exp(g_end - gc3)).reshape(n, ch).astype(BF16)
    v16 = v.astype(BF16)

    head_sum = _head_mean_matrix(HEAD_W, BF16)
    in_blk = row & (HGRN_EXACT - 1)
    out = jnp.dot((q * k).astype(BF16), head_sum, preferred_element_type=F32) * v
    for r in range(1, HGRN_EXACT):
        decay = jnp.exp(jnp.minimum(gc - toward_keys_in_block(gc, r), 0.0))
        w = jnp.where(reaches(in_blk, r, HGRN_EXACT), q * toward_keys_in_block(k, r) * decay, 0.0)
        out = out + jnp.dot(w.astype(BF16), head_sum, preferred_element_type=F32) * toward_keys_in_block(v, r)

    lane_head = _div_pow2(lax.broadcasted_iota(jnp.int32, (HGRN_CHUNK, ch), 1), HEAD_W)
    qi =lax.broadcasted_iota(jnp.int32, (HGRN_CHUNK, ch), 0)
    kj = lax.broadcasted_iota(jnp.int32, (HGRN_CHUNK, ch), 1) & (HEAD_W - 1)
    levels = []
    s = HGRN_CHUNK // 2
    while s >= HGRN_EXACT:
        g4 = gc.reshape(n // (2 * s), 2 * s, ch)
        edge = s if reverse else s - 1
        anchor = jnp.broadcast_to(g4[:, edge:edge + 1, :], g4.shape).reshape(n, ch)
        upper = (row & (2 * s - 1)) >= s
        is_query = jnp.logical_not(upper) if reverse else upper
        qs = jnp.where(is_query, q * jnp.exp(jnp.minimum(gc - anchor, 0.0)), 0.0).astype(BF16)
        ks = jnp.where(is_query, 0.0, k * jnp.exp(jnp.minimum(anchor - gc, 0.0))).astype(BF16)
        same_parent = _div_pow2(qi, 2 * s) == _div_pow2(kj, 2 * s)
        levels.append((qs, ks, same_parent))
        s //= 2

    def head_stack(t):
        return jnp.concatenate([jnp.where(lane_head == h, t, jnp.zeros_like(t)) for h in range(ch // HEAD_W)], axis=0)

    nt = (((1,), (1,)), ((), ()))
    tn = (((0,), (0,)), ((), ()))
    bd = _head_mean_matrix(HEAD_W, F32) > 0.5
    outs = []
    for b in range(bn):
        sl = slice(b * HGRN_CHUNK, (b + 1) * HGRN_CHUNK)
        scores = jnp.zeros((HGRN_CHUNK, ch), F32)
        for qs, ks, same_parent in levels:
            sc = lax.dot_general(qs[sl], head_stack(ks[sl]), nt, preferred_element_type=F32)
            scores = scores + jnp.where(same_parent, sc, 0.0)
        o_b = jnp.dot(scores.astype(BF16), head_stack(v16[sl]), preferred_element_type=F32)
        st = st_ref[b]
        o_b = o_b + lax.dot_general(q_state[sl], st.astype(BF16), nt, preferred_element_type=F32)
        upd = lax.dot_general(v16[sl], k_state[sl], tn, preferred_element_type=F32)
        st_ref[b] = st * jnp.exp(g_end[b]) + jnp.where(bd, upd, 0.0)
        outs.append(o_b)
    o_ref[...] = (out + jnp.concatenate(outs, axis=0)).reshape(bn, HGRN_CHUNK, ch)


def _hgrn(u, lb, reverse):
    bn, ltot, _ = u.shape
    n_chunks = ltot // HGRN_CHUNK
    n_ctx = CTX_LEN // HGRN_CHUNK

    def chunk(i):
        if not reverse:
            return i
        return jnp.where(i < n_ctx, n_ctx - 1 - i, n_chunks - 1 + n_ctx - i)

    col = lambda j: (lambda i: (0, chunk(i), j))
    return pl.pallas_call(
        functools.partial(_hgrn_kernel, reverse=reverse),
        grid=(n_chunks,),
        in_specs=[
            pl.BlockSpec((bn, HGRN_CHUNK, GROUP_W), col(2)),
            pl.BlockSpec((bn, HGRN_CHUNK, GROUP_W), col(3)),
            pl.BlockSpec((bn, HGRN_CHUNK, GROUP_W), col(5 if reverse else 4)),
            pl.BlockSpec((1, GROUP_W), lambda i: (0, 0)),
        ],
        out_specs=pl.BlockSpec((bn, HGRN_CHUNK, GROUP_W), lambda i: (0, chunk(i), 0)),
        out_shape=jax.ShapeDtypeStruct((bn, ltot, GROUP_W), F32),
        scratch_shapes=[pltpu.VMEM((bn, GROUP_W, GROUP_W), F32)],
        name="hgrn2_bwd" if reverse else "hgrn2_fwd",
    )(u, u, u, lb)


def _rope_kernel(q_ref, k_ref, v_ref, cos_ref, sin_ref, qo_ref, kt_ref, vo_ref):
    cos = cos_ref[...]
    sin = sin_ref[...]
    lane = lax.broadcasted_iota(jnp.int32, cos.shape, 1)
    first_half = (lane & (SUB_HEAD_W - 1)) < SUB_HEAD_W // 2

    def rope(x):
        partner = jnp.where(first_half, pltpu.roll(x, GROUP_W - SUB_HEAD_W // 2, 1),
                            pltpu.roll(x, SUB_HEAD_W // 2, 1))
        return x * cos + partner * sin

    qo_ref[0] = (rope(q_ref[0]) * (SUB_HEAD_W ** -0.5 * LOG2_E)).astype(BF16)
    kt_ref[0] = rope(k_ref[0]).T.astype(BF16)
    vo_ref[0] = v_ref[0].astype(BF16)


def _rope(u, cos_t, sin_t):
    bsz, ltot, _ = u.shape
    col = lambda j: (lambda b, t: (b, t, j))
    tok = pl.BlockSpec((1, TOK_TILE, GROUP_W), lambda b, t: (b, t, 0))
    tab = pl.BlockSpec((TOK_TILE, GROUP_W), lambda b, t: (t, 0))
    return pl.pallas_call(
        _rope_kernel,
        grid=(bsz, ltot // TOK_TILE),
        in_specs=[
            pl.BlockSpec((1, TOK_TILE, GROUP_W), col(10)),
            pl.BlockSpec((1, TOK_TILE, GROUP_W), col(11)),
            pl.BlockSpec((1, TOK_TILE, GROUP_W), col(12)),
            tab, tab,
        ],
        out_specs=[tok, pl.BlockSpec((1, GROUP_W, TOK_TILE), lambda b, t: (b, 0, t)), tok],
        out_shape=[
            jax.ShapeDtypeStruct((bsz, ltot, GROUP_W), BF16),
            jax.ShapeDtypeStruct((bsz, GROUP_W, ltot), BF16),
            jax.ShapeDtypeStruct((bsz, ltot, GROUP_W), BF16),
        ],
        name="rope_qkv",
    )(u, u, u, cos_t, sin_t)


def _attn_kernel(lamp_ref, g_ref, q_ref, kt_ref, v_ref, o_ref, s_scr, *, lam_init, t_off):
    t = pl.program_id(1) + t_off
    ltot = v_ref.shape[1]
    tq = q_ref.shape[1]

    lp = lamp_ref[...]
    lam = (jnp.exp(jnp.sum(lp[0:1] * lp[1:2], axis=-1, keepdims=True))
           - jnp.exp(jnp.sum(lp[2:3] * lp[3:4], axis=-1, keepdims=True)) + lam_init)

    def attend(nk):
        q = q_ref[0]
        sub = _div_pow2(lax.broadcasted_iota(jnp.int32, q.shape, 1), SUB_HEAD_W)
        lane_head = _div_pow2(lax.broadcasted_iota(jnp.int32, (tq, GROUP_W), 1), HEAD_W)

        def scores(j, slot):
            lhs = jnp.concatenate([jnp.where(sub == 2 * j, q, jnp.zeros_like(q)),
                                   jnp.where(sub == 2 * j + 1, q, jnp.zeros_like(q))], axis=0)
            s_scr[slot, :, :nk] = jnp.dot(lhs, kt_ref[0, :, :nk], preferred_element_type=F32)

        def mix(j, slot, o):
            s = s_scr[slot, :, :nk]
            p = jnp.exp2(s - jnp.max(s, axis=-1, keepdims=True))
            l = jnp.sum(p, axis=-1, keepdims=True)
            a = p[:tq] - p[tq:] * (lam * l[:tq] / l[tq:])
            pv = jnp.dot(a.astype(BF16), v_ref[0, :nk, :], preferred_element_type=F32)
            return o + jnp.where(lane_head == j, pv * (1.0 / l[:tq]), 0.0)

        scores(0, 0)
        o = jnp.zeros((tq, GROUP_W), F32)
        for j in range(N_PAIRS):
            if j + 1 < N_PAIRS:
                scores(j + 1, (j + 1) % 2)
            o = mix(j, j % 2, o)
        o_ref[0] = _head_rms_norm(o, g_ref[...]) * (1.0 - lam_init)

    @pl.when(t == 0)
    def _():
        attend(CTX_LEN)

    @pl.when(t > 0)
    def _():
        attend(ltot)


def _attention(lam_params, norm_g, q_rot, k_t, v16, lam_init, with_ctx):
    bsz, ltot, _ = q_rot.shape
    t_off = 0 if with_ctx else CTX_LEN // TOK_TILE
    n_t = ltot // TOK_TILE - t_off
    return pl.pallas_call(
        functools.partial(_attn_kernel, lam_init=lam_init, t_off=t_off),
        grid=(bsz, n_t),
        in_specs=[
            pl.BlockSpec(lam_params.shape, lambda b, t: (0, 0)),
            pl.BlockSpec(norm_g.shape, lambda b, t: (0, 0)),
            pl.BlockSpec((1, TOK_TILE, GROUP_W), lambda b, t: (b, t + t_off, 0)),
            pl.BlockSpec((1, GROUP_W, ltot), lambda b, t: (b, 0, 0)),
            pl.BlockSpec((1, ltot, GROUP_W), lambda b, t: (b, 0, 0)),
        ],
        out_specs=pl.BlockSpec((1, TOK_TILE, GROUP_W), lambda b, t: (b, t, 0)),
        out_shape=jax.ShapeDtypeStruct((bsz, n_t * TOK_TILE, GROUP_W), F32),
        scratch_shapes=[pltpu.VMEM((2, 2 * TOK_TILE, ltot), F32)],
        name="diff_attention",
    )(lam_params, norm_g, q_rot, k_t, v16)


def _outproj_kernel(x_ref, mod_ref, gpost_ref, hgain_ref, w_ref, ga_ref, gb_ref, gc_ref, gd_ref,
                    ya_ref, ybf_ref, ybb_ref, yc_ref, yd_ref, o_ref, *, ctx_row, t_off):
    b = pl.program_id(0)
    t = pl.program_id(1) + t_off
    row = jnp.where(t == 0, ctx_row, b)
    gate = mod_ref[pl.ds(row, 1), :][:, 2 * D_MODEL:]
    ya = ya_ref[0, 0] + ya_ref[1, 0]
    yb = _head_rms_norm(ybf_ref[0] + ybb_ref[0], hgain_ref[...])
    groups = ((ya, ga_ref), (yb, gb_ref), (yc_ref[0], gc_ref), (yd_ref[0], gd_ref))
    acc = jnp.zeros((x_ref.shape[1], D_MODEL), F32)
    for j, (y, g_ref) in enumerate(groups):
        m = (y * _silu(g_ref[0])).astype(BF16)
        acc = acc + jnp.dot(m, w_ref[j * GROUP_W:(j + 1) * GROUP_W, :], preferred_element_type=F32)
    r = acc * lax.rsqrt(jnp.mean(acc * acc, axis=-1, keepdims=True) + RMS_EPS) * gpost_ref[...]
    o_ref[0] = x_ref[0] + gate * r


def _outproj(xs, mod, g_post, hgrn_gain, w_out, u, ya2, yb_f, yb_b, yc, yd, ctx_row, with_ctx):
    bsz, ltot, d = xs.shape
    t_off = 0 if with_ctx else CTX_LEN // TOK_TILE
    n_t = ltot // TOK_TILE - t_off
    assert yd.shape[1] == n_t * TOK_TILE
    col =lambda j: (lambda b, t: (b, t + t_off, j))
    tok = lambda b, t: (b, t + t_off, 0)
    two = lambda b, t: (0, b, t + t_off, 0)
    grp = pl.BlockSpec((1, TOK_TILE, GROUP_W), tok)
    return pl.pallas_call(
        functools.partial(_outproj_kernel, ctx_row=ctx_row, t_off=t_off),
        grid=(bsz, n_t),
        in_specs=[
            pl.BlockSpec((1, TOK_TILE, d), tok),
            pl.BlockSpec(mod.shape, lambda b, t: (0, 0)),
            pl.BlockSpec((1, d), lambda b, t: (0, 0)),
            pl.BlockSpec((1, GROUP_W), lambda b, t: (0, 0)),
            pl.BlockSpec(w_out.shape, lambda b, t: (0, 0)),
            pl.BlockSpec((1, TOK_TILE, GROUP_W), col(1)),
            pl.BlockSpec((1, TOK_TILE, GROUP_W), col(6)),
            pl.BlockSpec((1, TOK_TILE, GROUP_W), col(9)),
            pl.BlockSpec((1, TOK_TILE, GROUP_W), col(13)),
            pl.BlockSpec((2, 1, TOK_TILE, GROUP_W), two),
            grp, grp, grp,
            pl.BlockSpec((1, TOK_TILE, GROUP_W), lambda b, t: (b, t, 0)),
        ],
        out_specs=pl.BlockSpec((1, TOK_TILE, d), lambda b, t: (b, t, 0)),
        out_shape=jax.ShapeDtypeStruct((bsz, n_t * TOK_TILE, d), F32),
        name="out_projection",
    )(xs, mod, g_post, hgrn_gain, w_out, u, u, u, u, ya2, yb_f, yb_b, yc, yd)


def _rope_tables(n_latent):
    rows = n_latent // GRID_W
    row = jnp.repeat(jnp.arange(rows, dtype=F32), GRID_W)
    colp = jnp.tile(jnp.arange(GRID_W, dtype=F32), rows)
    n_freq = SUB_HEAD_W // 4
    inv_freq = ROPE_THETA ** (-jnp.arange(n_freq, dtype=F32) / n_freq)
    ang = jnp.concatenate([row[:, None] * inv_freq, colp[:, None] * inv_freq], axis=-1)
    cos, sin = jnp.cos(ang), jnp.sin(ang)
    n_sub = GROUP_W // SUB_HEAD_W
    cos_t = jnp.tile(jnp.concatenate([cos, cos], axis=-1), (1, n_sub))
    sin_t = jnp.tile(jnp.concatenate([-sin, sin], axis=-1), (1, n_sub))
    cos_t = jnp.concatenate([jnp.ones((CTX_LEN, GROUP_W), F32), cos_t], axis=0)
    sin_t = jnp.concatenate([jnp.zeros((CTX_LEN, GROUP_W), F32), sin_t], axis=0)
    return cos_t, sin_t


def _block_diag(w):
    return jax.scipy.linalg.block_diag(*[w[h] for h in range(w.shape[0])])


def kernel(x, c, ctx, c_ctx, w_mod, b_mod, g_pre, g_post, w_in, w_out, lru_conv_w, lru_conv_b, lru_w_r, lru_b_r, lru_w_i, lru_b_i, lru_lambda, hgrn_lb, hgrn_norm_g, conf_conv_w, conf_conv_b, conf_ln_g, conf_ln_b, diff_lam_q1, diff_lam_k1, diff_lam_q2, diff_lam_k2, diff_norm_g):
    bsz, n_latent, d = x.shape
    depth = w_mod.shape[0]
    assert d == D_MODEL and ctx.shape[1] == CTX_LEN and n_latent % TOK_TILE == 0
    assert CTX_LEN == TOK_TILE == TIME_CHUNK

    ctx_row = bsz
    mod_rows = -(-(bsz + 1) // SUBLANES) * SUBLANES
    cc = jnp.zeros((mod_rows, d), F32).at[:bsz].set(c).at[ctx_row].set(c_ctx)
    mod = _modulation(cc, w_mod, b_mod[:, None, :])
    lbs = _lower_bounds(hgrn_lb.reshape(depth, 2 * GROUP_W)).reshape(depth, 2, 1, GROUP_W)
    cos_t, sin_t = _rope_tables(n_latent)

    xs = jnp.concatenate([ctx, x], axis=1)
    for layer in range(depth):
        with_ctx = layer < depth - 1
        u = _inproj(xs, mod[layer], g_pre[layer][None], w_in[layer].astype(BF16), ctx_row)

        wg = jnp.stack([jnp.concatenate([_block_diag(lru_w_r[layer, dd]), _block_diag(lru_w_i[layer, dd])], axis=1)
                        for dd in range(2)]).astype(BF16)
        bg = jnp.concatenate([lru_b_r[layer], lru_b_i[layer]], axis=-1)[:, None, :]
        h_t = _lru(jnp.transpose(u[:, :, :GROUP_W], (1, 0, 2)), lru_conv_w[layer], lru_conv_b[layer][None],
                   wg, bg, lru_lambda[layer][:, None, :])
        ya2 = jnp.transpose(h_t, (0, 2, 1, 3))

        yc_t = _conformer(jnp.transpose(u[:, :, 7 * GROUP_W:9 * GROUP_W], (1, 0, 2)), conf_conv_w[layer],
                          conf_conv_b[layer][None], conf_ln_g[layer][None], conf_ln_b[layer][None])
        yc = jnp.transpose(yc_t, (1, 0, 2))

        yb_f = _hgrn(u, lbs[layer, 0], reverse=False)
        yb_b = _hgrn(u, lbs[layer, 1], reverse=True)

        lam_init = 0.8 - 0.6 * math.exp(-0.3 * layer)
        lam_params = jnp.stack([diff_lam_q1[layer], diff_lam_k1[layer], diff_lam_q2[layer], diff_lam_k2[layer]])
        q_rot, k_t, v16 = _rope(u, cos_t, sin_t)
        yd = _attention(lam_params, jnp.tile(diff_norm_g[layer], N_PAIRS)[None], q_rot, k_t, v16, lam_init, with_ctx)

        xs = _outproj(xs, mod[layer], g_post[layer][None], jnp.tile(hgrn_norm_g[layer], N_PAIRS)[None],
                      w_out[layer].astype(BF16), u, ya2, yb_f, yb_b, yc, yd, ctx_row, with_ctx)
    return xs
```

```python
import functools
import math

import jax
import jax.numpy as jnp
from jax import lax
from jax.experimental import pallas as pl
from jax.experimental.pallas import tpu as pltpu

F32 = jnp.float32
BF16 = jnp.bfloat16

D_MODEL = 1024
CTX_LEN = 256
GRID_W = 64
GROUP_W = 256
QKV_FIRST = 10
U_LRU_GATE, U_HGRN_GATE, U_CONF_GATE, U_DIFF_GATE = 1, 6, 9, 10
LRU_CONV = 4
LRU_C = 8.0
HEAD_W = 64
SUB_HEAD_W = 32
N_PAIRS = GROUP_W // HEAD_W
LB_FLOOR = 1e-30
CONV_K = 31
ROPE_THETA = 10000.0
RMS_EPS = 1e-6
LN_EPS = 1e-5
LOG2_E = math.log2(math.e)

SUBLANES = 8
TOK_TILE = 256
TIME_CHUNK = 256
CONF_HALO = 16
LRU_HALO_L = 2
LRU_HALO_R = 1
HGRN_CHUNK = 64
HGRN_EXACT = 4
MOD_N_TILE = 1024


def _sigmoid(x):
    return 1.0 / (1.0 + jnp.exp(-x))


def _silu(x):
    return x * _sigmoid(x)


def _softplus(x):
    return jnp.maximum(x, 0.0) + jnp.log1p(jnp.exp(-jnp.abs(x)))


def _div_pow2(x, n):
    assert n & (n - 1) == 0
    return x >> (n.bit_length() - 1)


def _head_mean_matrix(width, dtype):
    r = _div_pow2(lax.broadcasted_iota(jnp.int32, (GROUP_W, GROUP_W), 0), width)
    c = _div_pow2(lax.broadcasted_iota(jnp.int32, (GROUP_W, GROUP_W), 1), width)
    return (r == c).astype(dtype)


def _head_rms_norm(o, gain):
    ms = jnp.dot(o * o, _head_mean_matrix(HEAD_W, F32), precision=lax.Precision.HIGHEST,
                 preferred_element_type=F32) * (1.0 / HEAD_W)
    return o * lax.rsqrt(ms + RMS_EPS) * gain


def _mod_kernel(cc_ref, w_ref, b_ref, o_ref):
    s = _silu(cc_ref[...])
    o_ref[0] = jnp.dot(s, w_ref[0], precision=lax.Precision.HIGHEST,
                       preferred_element_type=F32) + b_ref[0]


def _modulation(cc, w_mod, b_mod):
    depth, d, n = w_mod.shape
    rows = cc.shape[0]
    return pl.pallas_call(
        _mod_kernel,
        grid=(depth, n // MOD_N_TILE),
        in_specs=[
            pl.BlockSpec((rows, d), lambda l, j: (0, 0)),
            pl.BlockSpec((1, d, MOD_N_TILE), lambda l, j: (l, 0, j)),
            pl.BlockSpec((1, 1, MOD_N_TILE), lambda l, j: (l, 0, j)),
        ],
        out_specs=pl.BlockSpec((1, rows, MOD_N_TILE), lambda l, j: (l, 0, j)),
        out_shape=jax.ShapeDtypeStruct((depth, rows, n), F32),
        name="modulation",
    )(cc, w_mod, b_mod)


def _lb_kernel(x_ref, o_ref):
    x = x_ref[...]
    e = jnp.exp(x - jnp.max(x, axis=0, keepdims=True))
    p = e / jnp.sum(e, axis=0, keepdims=True)
    acc = jnp.zeros_like(p[0:1])
    for l in range(x.shape[0]):
        if l > 0:
            acc = acc + p[l:l + 1]
        o_ref[l:l + 1, :] = acc


def _lower_bounds(lb_raw):
    return pl.pallas_call(
        _lb_kernel, out_shape=jax.ShapeDtypeStruct(lb_raw.shape, F32), name="hgrn_lower_bounds",
    )(lb_raw)


def _inproj_kernel(x_ref, mod_ref, g_ref, w_ref, wqkv_ref, cos_ref, sin_ref, u_ref, qt_ref, ko_ref, vt_ref,
                   *, ctx_row):
    b = pl.program_id(0)
    t = pl.program_id(1)
    row = jnp.where(t == 0, ctx_row, b)
    mod = mod_ref[pl.ds(row, 1), :]
    shift = mod[:, :D_MODEL]
    scale = mod[:, D_MODEL:2 * D_MODEL]
    x = x_ref[0]
    y = x * lax.rsqrt(jnp.mean(x * x, axis=-1, keepdims=True) + RMS_EPS) * g_ref[...]
    h = y * (1.0 + scale) + shift
    h16 = h.astype(BF16)
    u_ref[0] = jnp.dot(h16, w_ref[...], preferred_element_type=F32)
    qkv = jnp.dot(h16, wqkv_ref[...], preferred_element_type=F32)

    cos = cos_ref[...]
    sin = sin_ref[...]
    lane = lax.broadcasted_iota(jnp.int32, cos.shape, 1)
    first_half = (lane & (SUB_HEAD_W - 1)) < SUB_HEAD_W // 2

    def rope(v):
        partner = jnp.where(first_half, pltpu.roll(v, GROUP_W - SUB_HEAD_W // 2, 1),
                            pltpu.roll(v, SUB_HEAD_W // 2, 1))
        return v * cos + partner * sin

    qt_ref[0] = (rope(qkv[:, :GROUP_W]) * (SUB_HEAD_W ** -0.5 * LOG2_E)).T.astype(BF16)
    ko_ref[0] = rope(qkv[:, GROUP_W:2 * GROUP_W]).astype(BF16)
    vt_ref[0] = qkv[:, 2 * GROUP_W:].T.astype(BF16)


def _inproj(xs, mod, g_pre, w_main, w_qkv, cos_t, sin_t, ctx_row):
    bsz, ltot, d = xs.shape
    n_main = w_main.shape[1]
    tok = pl.BlockSpec((1, TOK_TILE, GROUP_W), lambda b, t: (b, t, 0))
    tok_t = pl.BlockSpec((1, GROUP_W, TOK_TILE), lambda b, t: (b, 0, t))
    tab = pl.BlockSpec((TOK_TILE, GROUP_W), lambda b, t: (t, 0))
    return pl.pallas_call(
        functools.partial(_inproj_kernel, ctx_row=ctx_row),
        grid=(bsz, ltot // TOK_TILE),
        in_specs=[
            pl.BlockSpec((1, TOK_TILE, d), lambda b, t: (b, t, 0)),
            pl.BlockSpec(mod.shape, lambda b, t: (0, 0)),
            pl.BlockSpec((1, d), lambda b, t: (0, 0)),
            pl.BlockSpec(w_main.shape, lambda b, t: (0, 0)),
            pl.BlockSpec(w_qkv.shape, lambda b, t: (0, 0)),
            tab, tab,
        ],
        out_specs=[pl.BlockSpec((1, TOK_TILE, n_main), lambda b, t: (b, t, 0)), tok_t, tok, tok_t],
        out_shape=[
            jax.ShapeDtypeStruct((bsz, ltot, n_main), F32),
            jax.ShapeDtypeStruct((bsz, GROUP_W, ltot), BF16),
            jax.ShapeDtypeStruct((bsz, ltot, GROUP_W), BF16),
            jax.ShapeDtypeStruct((bsz, GROUP_W, ltot), BF16),
        ],
        name="in_projection",
    )(xs, mod, g_pre, w_main, w_qkv, cos_t, sin_t)


def _lru_kernel(xm_ref, xp_ref, xn_ref, cw_ref, cb_ref, wg_ref, bg_ref, lam_ref, o_ref,
                a_scr, u_scr, h_scr, *, n_chunks):
    d = pl.program_id(0)
    i = pl.program_id(1)
    c = _lru_chunk(d, i, n_chunks)
    tc, bn, ch = xm_ref.shape

    @pl.when(i == 0)
    def _():
        h_scr[...] = jnp.zeros_like(h_scr)

    left_ok = c >= 2
    right_ok = jnp.logical_and(c >= 1, c <= n_chunks - 2)
    xp = jnp.where(left_ok, xp_ref[...], 0.0)
    xn = jnp.where(right_ok, xn_ref[...], 0.0)
    xe = jnp.concatenate([xp, xm_ref[...], xn], axis=0)
    xc = jnp.zeros((tc, bn, ch), F32) + cb_ref[...].reshape(1, 1, ch)
    for k in range(LRU_CONV):
        xc = xc + cw_ref[k:k + 1, :].reshape(1, 1, ch) * xe[k:k + tc]

    x2 = xc.reshape(tc * bn, ch)
    gates = jnp.dot(x2.astype(BF16), wg_ref[0], preferred_element_type=F32) + bg_ref[0]
    r = _sigmoid(gates[:, :ch])
    ig = _sigmoid(gates[:, ch:])
    log_a = -LRU_C * r * _softplus(-lam_ref[0])
    a = jnp.exp(log_a)
    uu = jnp.sqrt(jnp.maximum(1.0 - a * a, 0.0)) * (ig * x2)
    a_scr[...] = a.reshape(tc, bn, ch)
    u_scr[...] = uu.reshape(tc, bn, ch)

    def step(j, h):
        t = jnp.where(d == 0, j, tc - 1 - j)
        h = a_scr[t] * h + u_scr[t]
        o_ref[0, t] = h
        return h

    h_scr[...] = lax.fori_loop(0, tc, step, h_scr[...], unroll=SUBLANES)


def _lru_chunk(d, i, n_chunks):
    return jnp.where(d == 0, i, jnp.where(i == 0, 0, n_chunks - i))


def _lru(u0t, conv_w, conv_b, wg, bg, lam):
    ltot, bn, ch = u0t.shape
    n_chunks = ltot // TIME_CHUNK
    main = lambda d, i: (_lru_chunk(d, i, n_chunks), 0, 0)
    prev = lambda d, i: (jnp.maximum(_lru_chunk(d, i, n_chunks) * (TIME_CHUNK // LRU_HALO_L) - 1, 0), 0, 0)
    nxt = lambda d, i: (jnp.minimum((_lru_chunk(d, i, n_chunks) + 1) * TIME_CHUNK, ltot - 1), 0, 0)
    return pl.pallas_call(
        functools.partial(_lru_kernel, n_chunks=n_chunks),
        grid=(2, n_chunks),
        in_specs=[
            pl.BlockSpec((TIME_CHUNK, bn, ch), main),
            pl.BlockSpec((LRU_HALO_L, bn, ch), prev),
            pl.BlockSpec((LRU_HALO_R, bn, ch), nxt),
            pl.BlockSpec(conv_w.shape, lambda d, i: (0, 0)),
            pl.BlockSpec(conv_b.shape, lambda d, i: (0, 0)),
            pl.BlockSpec((1,) + wg.shape[1:], lambda d, i: (d, 0, 0)),
            pl.BlockSpec((1,) + bg.shape[1:], lambda d, i: (d, 0, 0)),
            pl.BlockSpec((1,) + lam.shape[1:], lambda d, i: (d, 0, 0)),
        ],
        out_specs=pl.BlockSpec((1, TIME_CHUNK, bn, ch),
                               lambda d, i: (d, _lru_chunk(d, i, n_chunks), 0, 0)),
        out_shape=jax.ShapeDtypeStruct((2, ltot, bn, ch), F32),
        scratch_shapes=[
            pltpu.VMEM((TIME_CHUNK, bn, ch), F32),
            pltpu.VMEM((TIME_CHUNK, bn, ch), F32),
            pltpu.VMEM((bn, ch), F32),
        ],
        name="rglru",
    )(u0t, u0t, u0t, conv_w, conv_b, wg, bg, lam)


def _conf_kernel(xm_ref, xp_ref, xn_ref, cw_ref, cb_ref, lg_ref, lb_ref, o_ref, y_scr, *, n_chunks):
    c = pl.program_id(0)
    tc, bn, _ = xm_ref.shape
    ch = GROUP_W

    def glu(x):
        return x[..., :ch] * _sigmoid(x[..., ch:])

    left_ok = c >= 2
    right_ok = jnp.logical_and(c >= 1, c <= n_chunks - 2)
    y_scr[0:CONF_HALO] = jnp.where(left_ok, glu(xp_ref[...]), 0.0)
    y_scr[CONF_HALO:CONF_HALO + tc] = glu(xm_ref[...])
    y_scr[CONF_HALO + tc:] = jnp.where(right_ok, glu(xn_ref[...]), 0.0)

    acc = jnp.zeros((tc, bn, ch), F32) + cb_ref[...].reshape(1, 1, ch)
    first = CONF_HALO - CONV_K // 2
    for k in range(CONV_K):
        acc = acc + cw_ref[k:k + 1, :].reshape(1, 1, ch) * y_scr[first + k:first + k + tc]

    z = acc.reshape(tc * bn, ch)
    mu = jnp.mean(z, axis=-1, keepdims=True)
    zc = z - mu
    var = jnp.mean(zc * zc, axis=-1, keepdims=True)
    y = zc * lax.rsqrt(var + LN_EPS) * lg_ref[...] + lb_ref[...]
    o_ref[...] = _silu(y).reshape(tc, bn, ch)


def _conformer(u78t, conv_w, conv_b, ln_g, ln_b):
    ltot, bn, ch2 = u78t.shape
    n_chunks = ltot // TIME_CHUNK
    per = TIME_CHUNK // CONF_HALO
    return pl.pallas_call(
        functools.partial(_conf_kernel, n_chunks=n_chunks),
        grid=(n_chunks,),
        in_specs=[
            pl.BlockSpec((TIME_CHUNK, bn, ch2), lambda c: (c, 0, 0)),
            pl.BlockSpec((CONF_HALO, bn, ch2), lambda c: (jnp.maximum(c * per - 1, 0), 0, 0)),
            pl.BlockSpec((CONF_HALO, bn, ch2), lambda c: (jnp.minimum((c + 1) * per, n_chunks * per - 1), 0, 0)),
            pl.BlockSpec(conv_w.shape, lambda c: (0, 0)),
            pl.BlockSpec(conv_b.shape, lambda c: (0, 0)),
            pl.BlockSpec(ln_g.shape, lambda c: (0, 0)),
            pl.BlockSpec(ln_b.shape, lambda c: (0, 0)),
        ],
        out_specs=pl.BlockSpec((TIME_CHUNK, bn, GROUP_W), lambda c: (c, 0, 0)),
        out_shape=jax.ShapeDtypeStruct((ltot, bn, GROUP_W), F32),
        scratch_shapes=[pltpu.VMEM((TIME_CHUNK + 2 * CONF_HALO, bn, GROUP_W), F32)],
        name="conformer_conv",
    )(u78t, u78t, u78t, conv_w, conv_b, ln_g, ln_b)


def _hgrn_kernel(q_ref, v_ref, z_ref, lb_ref, o_ref, st_ref, *, reverse):
    i = pl.program_id(0)
    bn = q_ref.shape[0]
    ch = GROUP_W
    n = bn * HGRN_CHUNK

    @pl.when(i == 0)
    def _():
        st_ref[...] = jnp.zeros_like(st_ref)

    def toward_keys(t, r):
        return pltpu.roll(t, (n - r) if reverse else r, 0)

    def toward_keys_in_block(t, r):
        t3 = t.reshape(n // SUBLANES, SUBLANES, ch)
        return pltpu.roll(t3, (SUBLANES - r) if reverse else r, 1).reshape(n, ch)

    def reaches(pos, r, size):
        return (pos < size - r) if reverse else (pos >= r)

    q = _silu(q_ref[...].reshape(n, ch))
    v = v_ref[...].reshape(n, ch)
    z = z_ref[...].reshape(n, ch)
    lb = lb_ref[...]
    f = jnp.maximum(lb, LB_FLOOR) + (1.0 - lb) * _sigmoid(z)
    g = jnp.log(f)
    k = 1.0 - f

    row = lax.broadcasted_iota(jnp.int32, (n, ch), 0)
    in_chunk = row & (HGRN_CHUNK - 1)
    gc = g
    sft = 1
    while sft < HGRN_CHUNK:
        gc = gc + jnp.where(reaches(in_chunk, sft, HGRN_CHUNK), toward_keys(gc, sft), 0.0)
        sft *= 2

    gc3 = gc.reshape(bn, HGRN_CHUNK, ch)
    last = 0 if reverse else HGRN_CHUNK - 1
    g_end = gc3[:, last:last + 1, :]
    q_state = (q * jnp.exp(gc)).astype(BF16)
    k_state = (k.reshape(bn, HGRN_CHUNK, ch) * jnp.exp(g_end - gc3)).reshape(n, ch).astype(BF16)
    v16 = v.astype(BF16)

    head_sum = _head_mean_matrix(HEAD_W, BF16)
    in_blk = row & (HGRN_EXACT - 1)
    out = jnp.dot((q * k).astype(BF16), head_sum, preferred_element_type=F32) * v
    for r in range(1, HGRN_EXACT):
        decay = jnp.exp(jnp.minimum(gc - toward_keys_in_block(gc, r), 0.0))
        w = jnp.where(reaches(in_blk, r, HGRN_EXACT), q * toward_keys_in_block(k, r) * decay, 0.0)
        out = out + jnp.dot(w.astype(BF16), head_sum, preferred_element_type=F32) * toward_keys_in_block(v, r)

    lane_head = _div_pow2(lax.broadcasted_iota(jnp.int32, (HGRN_CHUNK, ch), 1), HEAD_W)
    qi =lax.broadcasted_iota(jnp.int32, (HGRN_CHUNK, ch), 0)
    kj = lax.broadcasted_iota(jnp.int32, (HGRN_CHUNK, ch), 1) & (HEAD_W - 1)
    levels = []
    s = HGRN_CHUNK // 2
    while s >= HGRN_EXACT:
        g4 = gc.reshape(n // (2 * s), 2 * s, ch)
        edge = s if reverse else s - 1
        anchor = jnp.broadcast_to(g4[:, edge:edge + 1, :], g4.shape).reshape(n, ch)
        upper = (row & (2 * s - 1)) >= s
        is_query = jnp.logical_not(upper) if reverse else upper
        qs = jnp.where(is_query, q * jnp.exp(jnp.minimum(gc - anchor, 0.0)), 0.0).astype(BF16)
        ks = jnp.where(is_query, 0.0, k * jnp.exp(jnp.minimum(anchor - gc, 0.0))).astype(BF16)
        same_parent = _div_pow2(qi, 2 * s) == _div_pow2(kj, 2 * s)
        levels.append((qs, ks, same_parent))
        s //= 2

    def head_stack(t):
        return jnp.concatenate([jnp.where(lane_head == h, t, jnp.zeros_like(t)) for h in range(ch // HEAD_W)], axis=0)

    nt = (((1,), (1,)), ((), ()))
    tn = (((0,), (0,)), ((), ()))
    bd = _head_mean_matrix(HEAD_W, F32) > 0.5
    outs = []
    for b in range(bn):
        sl = slice(b * HGRN_CHUNK, (b + 1) * HGRN_CHUNK)
        scores = jnp.zeros((HGRN_CHUNK, ch), F32)
        for qs, ks, same_parent in levels:
            sc = lax.dot_general(qs[sl], head_stack(ks[sl]), nt, preferred_element_type=F32)
            scores = scores + jnp.where(same_parent, sc, 0.0)
        o_b = jnp.dot(scores.astype(BF16), head_stack(v16[sl]), preferred_element_type=F32)
        st = st_ref[b]
        o_b = o_b + lax.dot_general(q_state[sl], st.astype(BF16), nt, preferred_element_type=F32)
        upd = lax.dot_general(v16[sl], k_state[sl], tn, preferred_element_type=F32)
        st_ref[b] = st * jnp.exp(g_end[b]) + jnp.where(bd, upd, 0.0)
        outs.append(o_b)
    o_ref[...] = (out + jnp.concatenate(outs, axis=0)).reshape(bn, HGRN_CHUNK, ch)


def _hgrn(u, lb, reverse):
    bn, ltot, _ = u.shape
    n_chunks = ltot // HGRN_CHUNK
    n_ctx = CTX_LEN // HGRN_CHUNK

    def chunk(i):
        if not reverse:
            return i
        return jnp.where(i < n_ctx, n_ctx - 1 - i, n_chunks - 1 + n_ctx - i)

    col = lambda j: (lambda i: (0, chunk(i), j))
    return pl.pallas_call(
        functools.partial(_hgrn_kernel, reverse=reverse),
        grid=(n_chunks,),
        in_specs=[
            pl.BlockSpec((bn, HGRN_CHUNK, GROUP_W), col(2)),
            pl.BlockSpec((bn, HGRN_CHUNK, GROUP_W), col(3)),
            pl.BlockSpec((bn, HGRN_CHUNK, GROUP_W), col(5 if reverse else 4)),
            pl.BlockSpec((1, GROUP_W), lambda i: (0, 0)),
        ],
        out_specs=pl.BlockSpec((bn, HGRN_CHUNK, GROUP_W), lambda i: (0, chunk(i), 0)),
        out_shape=jax.ShapeDtypeStruct((bn, ltot, GROUP_W), F32),
        scratch_shapes=[pltpu.VMEM((bn, GROUP_W, GROUP_W), F32)],
        name="hgrn2_bwd" if reverse else "hgrn2_fwd",
    )(u, u, u, lb)


def _attn_kernel(lamp_ref, g_ref, qt_ref, k_ref, vt_ref, o_ref, s_scr, *, lam_init, q_start, nk):
    tq = TOK_TILE
    n_sub = o_ref.shape[1] // tq

    lp = lamp_ref[...]
    lam = (jnp.exp(jnp.sum(lp[0:1] * lp[1:2], axis=-1, keepdims=True))
           - jnp.exp(jnp.sum(lp[2:3] * lp[3:4], axis=-1, keepdims=True)) + lam_init)
    sub = _div_pow2(lax.broadcasted_iota(jnp.int32, (GROUP_W, tq), 0), SUB_HEAD_W)

    def scores(i, j):
        qt = qt_ref[0, :, pl.ds(pl.multiple_of(q_start + i * tq, tq), tq)]
        w = jnp.concatenate([jnp.where(sub == 2 * j, qt, jnp.zeros_like(qt)),
                             jnp.where(sub == 2 * j + 1, qt, jnp.zeros_like(qt))], axis=1)
        s = jnp.dot(k_ref[0, :nk, :], w, preferred_element_type=F32)
        s_scr[j % 2, :nk, :] = s
        return jnp.max(s, axis=0, keepdims=True)

    def mix(j, m):
        p = jnp.exp2(s_scr[j % 2, :nk, :] - m)
        l = jnp.sum(p, axis=0, keepdims=True)
        a = (p[:, :tq] - p[:, tq:] * (lam * l[:, :tq] / l[:, tq:])).astype(BF16)
        ot = jnp.dot(vt_ref[0, j * HEAD_W:(j + 1) * HEAD_W, :nk], a, preferred_element_type=F32)
        return ot * (1.0 / l[:, :tq])

    def tile(i, m):
        heads = []
        for j in range(N_PAIRS):
            if j + 1 < N_PAIRS:
                m_next = scores(i, j + 1)
            else:
                m_next = scores(jnp.minimum(i + 1, n_sub - 1), 0)
            heads.append(mix(j, m))
            m = m_next
        ot = jnp.stack(heads)
        ms = jnp.mean(ot * ot, axis=1, keepdims=True)
        y = ot * lax.rsqrt(ms + RMS_EPS) * g_ref[...].reshape(1, HEAD_W, 1) * (1.0 - lam_init)
        o_ref[0, pl.ds(pl.multiple_of(i * tq, tq), tq), :] = y.reshape(GROUP_W, tq).T
        return m

    lax.fori_loop(0, n_sub, tile, scores(0, 0))


def _attention(lam_params, norm_g, q_t, k_rot, v_t, lam_init, q_start, n_q, nk):
    bsz, ltot, _ = k_rot.shape
    return pl.pallas_call(
        functools.partial(_attn_kernel, lam_init=lam_init, q_start=q_start, nk=nk),
        grid=(bsz,),
        in_specs=[
            pl.BlockSpec(lam_params.shape, lambda b: (0, 0)),
            pl.BlockSpec(norm_g.shape, lambda b: (0, 0)),
            pl.BlockSpec((1, GROUP_W, ltot), lambda b: (b, 0, 0)),
            pl.BlockSpec((1, ltot, GROUP_W), lambda b: (b, 0, 0)),
            pl.BlockSpec((1, GROUP_W, ltot), lambda b: (b, 0, 0)),
        ],
        out_specs=pl.BlockSpec((1, n_q, GROUP_W), lambda b: (b, 0, 0)),
        out_shape=jax.ShapeDtypeStruct((bsz, n_q, GROUP_W), F32),
        scratch_shapes=[pltpu.VMEM((2, nk, 2 * TOK_TILE), F32)],
        name="diff_attention",
    )(lam_params, norm_g, q_t, k_rot, v_t)


def _outproj_kernel(x_ref, mod_ref, gpost_ref, hgain_ref, w_ref, ga_ref, gb_ref, gc_ref, gd_ref,
                    ya_ref, ybf_ref, ybb_ref, yc_ref, yd_ref, ydc_ref, o_ref, *, ctx_row, t_off):
    b = pl.program_id(0)
    t = pl.program_id(1) + t_off
    row = jnp.where(t == 0, ctx_row, b)
    gate = mod_ref[pl.ds(row, 1), :][:, 2 * D_MODEL:]
    ya = ya_ref[0, 0] + ya_ref[1, 0]
    yb = _head_rms_norm(ybf_ref[0] + ybb_ref[0], hgain_ref[...])
    yd = jnp.where(t == 0, ydc_ref[0], yd_ref[0])
    groups = ((ya, ga_ref), (yb, gb_ref), (yc_ref[0], gc_ref), (yd, gd_ref))
    acc = jnp.zeros((x_ref.shape[1], D_MODEL), F32)
    for j, (y, g_ref) in enumerate(groups):
        m = (y * _silu(g_ref[0])).astype(BF16)
        acc = acc + jnp.dot(m, w_ref[j * GROUP_W:(j + 1) * GROUP_W, :], preferred_element_type=F32)
    r = acc * lax.rsqrt(jnp.mean(acc * acc, axis=-1, keepdims=True) + RMS_EPS) * gpost_ref[...]
    o_ref[0] = x_ref[0] + gate * r


def _outproj(xs, mod, g_post, hgrn_gain, w_out, u, ya2, yb_f, yb_b, yc, yd, yd_ctx, ctx_row, with_ctx):
    bsz, ltot, d = xs.shape
    t_off = 0 if with_ctx else CTX_LEN // TOK_TILE
    n_t = ltot // TOK_TILE - t_off
    col =lambda j: (lambda b, t: (b, t + t_off, j))
    tok = lambda b, t: (b, t + t_off, 0)
    two = lambda b, t: (0, b, t + t_off, 0)
    grp = pl.BlockSpec((1, TOK_TILE, GROUP_W), tok)
    return pl.pallas_call(
        functools.partial(_outproj_kernel, ctx_row=ctx_row, t_off=t_off),
        grid=(bsz, n_t),
        in_specs=[
            pl.BlockSpec((1, TOK_TILE, d), tok),
            pl.BlockSpec(mod.shape, lambda b, t: (0, 0)),
            pl.BlockSpec((1, d), lambda b, t: (0, 0)),
            pl.BlockSpec((1, GROUP_W), lambda b, t: (0, 0)),
            pl.BlockSpec(w_out.shape, lambda b, t: (0, 0)),
            pl.BlockSpec((1, TOK_TILE, GROUP_W), col(U_LRU_GATE)),
            pl.BlockSpec((1, TOK_TILE, GROUP_W), col(U_HGRN_GATE)),
            pl.BlockSpec((1, TOK_TILE, GROUP_W), col(U_CONF_GATE)),
            pl.BlockSpec((1, TOK_TILE, GROUP_W), col(U_DIFF_GATE)),
            pl.BlockSpec((2, 1, TOK_TILE, GROUP_W), two),
            grp, grp, grp,
            pl.BlockSpec((1, TOK_TILE, GROUP_W), lambda b, t: (b, jnp.maximum(t + t_off - 1, 0), 0)),
            pl.BlockSpec((1, TOK_TILE, GROUP_W), lambda b, t: (b, 0, 0)),
        ],
        out_specs=pl.BlockSpec((1, TOK_TILE, d), lambda b, t: (b, t, 0)),
        out_shape=jax.ShapeDtypeStruct((bsz, n_t * TOK_TILE, d), F32),
        name="out_projection",
    )(xs, mod, g_post, hgrn_gain, w_out, u, u, u, u, ya2, yb_f, yb_b, yc, yd, yd_ctx)


def _rope_tables(n_latent):
    rows = n_latent // GRID_W
    row = jnp.repeat(jnp.arange(rows, dtype=F32), GRID_W)
    colp = jnp.tile(jnp.arange(GRID_W, dtype=F32), rows)
    n_freq = SUB_HEAD_W // 4
    inv_freq = ROPE_THETA ** (-jnp.arange(n_freq, dtype=F32) / n_freq)
    ang = jnp.concatenate([row[:, None] * inv_freq, colp[:, None] * inv_freq], axis=-1)
    cos, sin = jnp.cos(ang), jnp.sin(ang)
    n_sub = GROUP_W // SUB_HEAD_W
    cos_t = jnp.tile(jnp.concatenate([cos, cos], axis=-1), (1, n_sub))
    sin_t = jnp.tile(jnp.concatenate([-sin, sin], axis=-1), (1, n_sub))
    cos_t = jnp.concatenate([jnp.ones((CTX_LEN, GROUP_W), F32), cos_t], axis=0)
    sin_t = jnp.concatenate([jnp.zeros((CTX_LEN, GROUP_W), F32), sin_t], axis=0)
    return cos_t, sin_t


def _block_diag(w):
    return jax.scipy.linalg.block_diag(*[w[h] for h in range(w.shape[0])])


def kernel(x, c, ctx, c_ctx, w_mod, b_mod, g_pre, g_post, w_in, w_out, lru_conv_w, lru_conv_b, lru_w_r, lru_b_r, lru_w_i, lru_b_i, lru_lambda, hgrn_lb, hgrn_norm_g, conf_conv_w, conf_conv_b, conf_ln_g, conf_ln_b, diff_lam_q1, diff_lam_k1, diff_lam_q2, diff_lam_k2, diff_norm_g):
    bsz, n_latent, d = x.shape
    depth = w_mod.shape[0]
    assert d == D_MODEL and ctx.shape[1] == CTX_LEN and n_latent % TOK_TILE == 0
    assert CTX_LEN == TOK_TILE == TIME_CHUNK

    ctx_row = bsz
    mod_rows = -(-(bsz + 1) // SUBLANES) * SUBLANES
    cc = jnp.zeros((mod_rows, d), F32).at[:bsz].set(c).at[ctx_row].set(c_ctx)
    mod = _modulation(cc, w_mod, b_mod[:, None, :])
    lbs = _lower_bounds(hgrn_lb.reshape(depth, 2 * GROUP_W)).reshape(depth, 2, 1, GROUP_W)
    cos_t, sin_t = _rope_tables(n_latent)

    xs = jnp.concatenate([ctx, x], axis=1)
    for layer in range(depth):
        with_ctx = layer < depth - 1
        w16 = w_in[layer].astype(BF16)
        w_main = jnp.concatenate([w16[:, :QKV_FIRST * GROUP_W], w16[:, (QKV_FIRST + 3) * GROUP_W:]], axis=1)
        w_qkv = w16[:, QKV_FIRST * GROUP_W:(QKV_FIRST + 3) * GROUP_W]
        u, q_t, k_rot, v_t = _inproj(xs, mod[layer], g_pre[layer][None], w_main, w_qkv, cos_t, sin_t, ctx_row)

        wg = jnp.stack([jnp.concatenate([_block_diag(lru_w_r[layer, dd]), _block_diag(lru_w_i[layer, dd])], axis=1)
                        for dd in range(2)]).astype(BF16)
        bg = jnp.concatenate([lru_b_r[layer], lru_b_i[layer]], axis=-1)[:, None, :]
        h_t = _lru(jnp.transpose(u[:, :, :GROUP_W], (1, 0, 2)), lru_conv_w[layer], lru_conv_b[layer][None],
                   wg, bg, lru_lambda[layer][:, None, :])
        ya2 = jnp.transpose(h_t, (0, 2, 1, 3))

        yc_t = _conformer(jnp.transpose(u[:, :, 7 * GROUP_W:9 * GROUP_W], (1, 0, 2)), conf_conv_w[layer],
                          conf_conv_b[layer][None], conf_ln_g[layer][None], conf_ln_b[layer][None])
        yc = jnp.transpose(yc_t, (1, 0, 2))

        yb_f = _hgrn(u, lbs[layer, 0], reverse=False)
        yb_b = _hgrn(u, lbs[layer, 1], reverse=True)

        lam_init = 0.8 - 0.6 * math.exp(-0.3 * layer)
        lam_params = jnp.stack([diff_lam_q1[layer], diff_lam_k1[layer], diff_lam_q2[layer], diff_lam_k2[layer]])
        att = functools.partial(_attention, lam_params, diff_norm_g[layer][:, None], q_t, k_rot, v_t, lam_init)
        yd = att(CTX_LEN, n_latent, CTX_LEN + n_latent)
        yd_ctx = att(0, CTX_LEN, CTX_LEN) if with_ctx else yd

        xs = _outproj(xs, mod[layer], g_post[layer][None], jnp.tile(hgrn_norm_g[layer], N_PAIRS)[None],
                      w_out[layer].astype(BF16), u, ya2, yb_f, yb_b, yc, yd, yd_ctx, ctx_row, with_ctx)
    return xs
```

```python
import functools
import math

import jax
import jax.numpy as jnp
from jax import lax
from jax.experimental import pallas as pl
from jax.experimental.pallas import tpu as pltpu

F32 = jnp.float32
BF16 = jnp.bfloat16

D_MODEL = 1024
CTX_LEN = 256
GRID_W = 64
GROUP_W = 256
QKV_FIRST = 10
U_LRU_GATE, U_HGRN_GATE, U_CONF_GATE, U_DIFF_GATE = 1, 6, 9, 10
LRU_CONV = 4
LRU_C = 8.0
HEAD_W = 64
SUB_HEAD_W = 32
N_PAIRS = GROUP_W // HEAD_W
LB_FLOOR = 1e-30
CONV_K = 31
ROPE_THETA = 10000.0
RMS_EPS = 1e-6
LN_EPS = 1e-5
LOG2_E = math.log2(math.e)

SUBLANES = 8
TOK_TILE = 256
CONF_HALO = 16
HGRN_CHUNK = 64
HGRN_EXACT = 4
MOD_N_TILE = 1024
ATT_SUM_ROWS = 16


def _sigmoid(x):
    return 1.0 / (1.0 + jnp.exp(-x))


def _silu(x):
    return x * _sigmoid(x)


def _softplus(x):
    return jnp.maximum(x, 0.0) + jnp.log1p(jnp.exp(-jnp.abs(x)))


def _div_pow2(x, n):
    assert n & (n - 1) == 0
    return x >> (n.bit_length() - 1)


def _head_mean_matrix(width, dtype):
    r = _div_pow2(lax.broadcasted_iota(jnp.int32, (GROUP_W, GROUP_W), 0), width)
    c = _div_pow2(lax.broadcasted_iota(jnp.int32, (GROUP_W, GROUP_W), 1), width)
    return (r == c).astype(dtype)


def _head_rms_norm(o, gain):
    ms = jnp.dot(o * o, _head_mean_matrix(HEAD_W, F32), precision=lax.Precision.HIGHEST,
                 preferred_element_type=F32) * (1.0 / HEAD_W)
    return o * lax.rsqrt(ms + RMS_EPS) * gain


def _mod_kernel(cc_ref, w_ref, b_ref, o_ref):
    s = _silu(cc_ref[...])
    o_ref[0] = jnp.dot(s, w_ref[0], precision=lax.Precision.HIGHEST,
                       preferred_element_type=F32) + b_ref[0]


def _modulation(cc, w_mod, b_mod):
    depth, d, n = w_mod.shape
    rows = cc.shape[0]
    return pl.pallas_call(
        _mod_kernel,
        grid=(depth, n // MOD_N_TILE),
        in_specs=[
            pl.BlockSpec((rows, d), lambda l, j: (0, 0)),
            pl.BlockSpec((1, d, MOD_N_TILE), lambda l, j: (l, 0, j)),
            pl.BlockSpec((1, 1, MOD_N_TILE), lambda l, j: (l, 0, j)),
        ],
        out_specs=pl.BlockSpec((1, rows, MOD_N_TILE), lambda l, j: (l, 0, j)),
        out_shape=jax.ShapeDtypeStruct((depth, rows, n), F32),
        name="modulation",
    )(cc, w_mod, b_mod)


def _lb_kernel(x_ref, o_ref):
    x = x_ref[...]
    e = jnp.exp(x - jnp.max(x, axis=0, keepdims=True))
    p = e / jnp.sum(e, axis=0, keepdims=True)
    acc = jnp.zeros_like(p[0:1])
    for l in range(x.shape[0]):
        if l > 0:
            acc = acc + p[l:l + 1]
        o_ref[l:l + 1, :] = acc


def _lower_bounds(lb_raw):
    return pl.pallas_call(
        _lb_kernel, out_shape=jax.ShapeDtypeStruct(lb_raw.shape, F32), name="hgrn_lower_bounds",
    )(lb_raw)


def _inproj_kernel(x_ref, mod_ref, g_ref, w_ref, wqkv_ref, cos_ref, sin_ref, u_ref, qt_ref, ko_ref, vt_ref,
                   *, ctx_row):
    b = pl.program_id(0)
    t = pl.program_id(1)
    row = jnp.where(t == 0, ctx_row, b)
    mod = mod_ref[pl.ds(row, 1), :]
    shift = mod[:, :D_MODEL]
    scale = mod[:, D_MODEL:2 * D_MODEL]
    x = x_ref[0]
    y = x * lax.rsqrt(jnp.mean(x * x, axis=-1, keepdims=True) + RMS_EPS) * g_ref[...]
    h = y * (1.0 + scale) + shift
    h16 = h.astype(BF16)
    u_ref[0] = jnp.dot(h16, w_ref[...], preferred_element_type=F32)
    qkv = jnp.dot(h16, wqkv_ref[...], preferred_element_type=F32)

    cos = cos_ref[...]
    sin = sin_ref[...]
    lane = lax.broadcasted_iota(jnp.int32, cos.shape, 1)
    first_half = (lane & (SUB_HEAD_W - 1)) < SUB_HEAD_W // 2

    def rope(v):
        partner = jnp.where(first_half, pltpu.roll(v, GROUP_W - SUB_HEAD_W // 2, 1),
                            pltpu.roll(v, SUB_HEAD_W // 2, 1))
        return v * cos + partner * sin

    qt_ref[0] = (rope(qkv[:, :GROUP_W]) * (SUB_HEAD_W ** -0.5 * LOG2_E)).T.astype(BF16)
    ko_ref[0] = rope(qkv[:, GROUP_W:2 * GROUP_W]).astype(BF16)
    vt_ref[0] = qkv[:, 2 * GROUP_W:].T.astype(BF16)


def _inproj(xs, mod, g_pre, w_main, w_qkv, cos_t, sin_t, ctx_row):
    bsz, ltot, d = xs.shape
    n_main = w_main.shape[1]
    tok = pl.BlockSpec((1, TOK_TILE, GROUP_W), lambda b, t: (b, t, 0))
    tok_t = pl.BlockSpec((1, GROUP_W, TOK_TILE), lambda b, t: (b, 0, t))
    tab = pl.BlockSpec((TOK_TILE, GROUP_W), lambda b, t: (t, 0))
    return pl.pallas_call(
        functools.partial(_inproj_kernel, ctx_row=ctx_row),
        grid=(bsz, ltot // TOK_TILE),
        in_specs=[
            pl.BlockSpec((1, TOK_TILE, d), lambda b, t: (b, t, 0)),
            pl.BlockSpec(mod.shape, lambda b, t: (0, 0)),
            pl.BlockSpec((1, d), lambda b, t: (0, 0)),
            pl.BlockSpec(w_main.shape, lambda b, t: (0, 0)),
            pl.BlockSpec(w_qkv.shape, lambda b, t: (0, 0)),
            tab, tab,
        ],
        out_specs=[pl.BlockSpec((1, TOK_TILE, n_main), lambda b, t: (b, t, 0)), tok_t, tok, tok_t],
        out_shape=[
            jax.ShapeDtypeStruct((bsz, ltot, n_main), F32),
            jax.ShapeDtypeStruct((bsz, GROUP_W, ltot), BF16),
            jax.ShapeDtypeStruct((bsz, ltot, GROUP_W), BF16),
            jax.ShapeDtypeStruct((bsz, GROUP_W, ltot), BF16),
        ],
        name="in_projection",
    )(xs, mod, g_pre, w_main, w_qkv, cos_t, sin_t)


def _segment_halo_ok(c, n_tiles):
    return c >= 2, jnp.logical_and(c >= 1, c <= n_tiles - 2)


def _lru_kernel(xm_ref, xp_ref, xn_ref, cw_ref, cb_ref, wg_ref, bg_ref, lam_ref, o_ref, h_scr, *, n_tiles):
    d = pl.program_id(0)
    i = pl.program_id(2)
    c = _lru_tile(d, i, n_tiles)
    t_rows, ch = xm_ref.shape[1], xm_ref.shape[2]

    @pl.when(i == 0)
    def _():
        h_scr[...] = jnp.zeros_like(h_scr)

    left_ok, right_ok = _segment_halo_ok(c, n_tiles)
    ext = jnp.concatenate([jnp.where(left_ok, xp_ref[0], 0.0), xm_ref[0], jnp.where(right_ok, xn_ref[0], 0.0)],
                          axis=0)
    xc = jnp.zeros((t_rows, ch), F32) + cb_ref[...]
    for k in range(LRU_CONV):
        off = SUBLANES - LRU_CONV // 2 + k
        xc = xc + cw_ref[k:k + 1, :] * ext[off:off + t_rows]

    gates = jnp.dot(xc.astype(BF16), wg_ref[0], preferred_element_type=F32) + bg_ref[0]
    r = _sigmoid(gates[:, :ch])
    ig = _sigmoid(gates[:, ch:])
    a = jnp.exp(-LRU_C * r * _softplus(-lam_ref[0]))
    uu = jnp.sqrt(jnp.maximum(1.0 - a * a, 0.0)) * (ig * xc)

    n_v = t_rows // SUBLANES
    pos = lax.broadcasted_iota(jnp.int32, (n_v, SUBLANES, ch), 1)

    def scan(reverse):
        av = a.reshape(n_v, SUBLANES, ch)
        uv = uu.reshape(n_v, SUBLANES, ch)
        sft = 1
        while sft < SUBLANES:
            ok = (pos < SUBLANES - sft) if reverse else (pos >= sft)
            roll_by = (SUBLANES - sft) if reverse else sft
            uv = jnp.where(ok, av * pltpu.roll(uv, roll_by, 1) + uv, uv)
            av = jnp.where(ok, av * pltpu.roll(av, roll_by, 1), av)
            sft *= 2
        carry = h_scr[...]
        order = range(n_v - 1, -1, -1) if reverse else range(n_v)
        edge = 0 if reverse else SUBLANES - 1
        for v in order:
            hv = uv[v] + av[v] * carry
            o_ref[0, 0, v * SUBLANES:(v + 1) * SUBLANES, :] = hv
            carry = hv[edge:edge + 1]
        h_scr[...] = carry

    @pl.when(d == 0)
    def _():
        scan(False)

    @pl.when(d == 1)
    def _():
        scan(True)


def _lru_tile(d, i, n_tiles):
    return jnp.where(d == 0, i, jnp.where(i == 0, 0, n_tiles - i))


def _lru(u, conv_w, conv_b, wg, bg, lam):
    bsz, ltot, _ = u.shape
    n_tiles = ltot // TOK_TILE
    per = TOK_TILE // SUBLANES
    ch = GROUP_W
    tile = lambda d, b, i: (b, _lru_tile(d, i, n_tiles), 0)
    prev = lambda d, b, i: (b, jnp.maximum(_lru_tile(d, i, n_tiles) * per - 1, 0), 0)
    nxt = lambda d, b, i: (b, jnp.minimum((_lru_tile(d, i, n_tiles) + 1) * per, n_tiles * per - 1), 0)
    return pl.pallas_call(
        functools.partial(_lru_kernel, n_tiles=n_tiles),
        grid=(2, bsz, n_tiles),
        in_specs=[
            pl.BlockSpec((1, TOK_TILE, ch), tile),
            pl.BlockSpec((1, SUBLANES, ch), prev),
            pl.BlockSpec((1, SUBLANES, ch), nxt),
            pl.BlockSpec(conv_w.shape, lambda d, b, i: (0, 0)),
            pl.BlockSpec(conv_b.shape, lambda d, b, i: (0, 0)),
            pl.BlockSpec((1,) + wg.shape[1:], lambda d, b, i: (d, 0, 0)),
            pl.BlockSpec((1,) + bg.shape[1:], lambda d, b, i: (d, 0, 0)),
            pl.BlockSpec((1,) + lam.shape[1:], lambda d, b, i: (d, 0, 0)),
        ],
        out_specs=pl.BlockSpec((1, 1, TOK_TILE, ch), lambda d, b, i: (d, b, _lru_tile(d, i, n_tiles), 0)),
        out_shape=jax.ShapeDtypeStruct((2, bsz, ltot, ch), F32),
        scratch_shapes=[pltpu.VMEM((1, ch), F32)],
        name="rglru",
    )(u, u, u, conv_w, conv_b, wg, bg, lam)


def _conf_kernel(vm_ref, gm_ref, vp_ref, gp_ref, vn_ref, gn_ref, cw_ref, cb_ref, lg_ref, lb_ref, o_ref, y_scr,
                 ys_scr, *, n_tiles):
    c = pl.program_id(1)
    t_rows, ch = vm_ref.shape[1], vm_ref.shape[2]
    left_ok, right_ok = _segment_halo_ok(c, n_tiles)

    def glu(v_ref, g_ref):
        return v_ref[0] * _sigmoid(g_ref[0])

    y_scr[0:CONF_HALO] = jnp.where(left_ok, glu(vp_ref, gp_ref), 0.0)
    y_scr[CONF_HALO:CONF_HALO + t_rows] = glu(vm_ref, gm_ref)
    y_scr[CONF_HALO + t_rows:] = jnp.where(right_ok, glu(vn_ref, gn_ref), 0.0)
    first = CONF_HALO - CONV_K // 2
    span = ys_scr.shape[1]
    acc = jnp.zeros((t_rows, ch), F32) + cb_ref[...]
    for j in range(SUBLANES):
        taps = [k for k in range(CONV_K) if (first + k) % SUBLANES == j]
        start = first + taps[0]
        ys_scr[j] = y_scr[start:start + span, :]
        for k in taps:
            lo = first + k - start
            acc = acc + cw_ref[k:k + 1, :] * ys_scr[j, lo:lo + t_rows, :]

    mu = jnp.mean(acc, axis=-1, keepdims=True)
    zc = acc - mu
    var = jnp.mean(zc * zc, axis=-1, keepdims=True)
    o_ref[0] = _silu(zc * lax.rsqrt(var + LN_EPS) * lg_ref[...] + lb_ref[...])


def _conformer(u, conv_w, conv_b, ln_g, ln_b):
    bsz, ltot, _ = u.shape
    n_tiles = ltot // TOK_TILE
    per = TOK_TILE // CONF_HALO
    ch = GROUP_W
    col_v, col_g = 7, 8
    tile = lambda j: (lambda b, c: (b, c, j))
    prev = lambda j: (lambda b, c: (b, jnp.maximum(c * per - 1, 0), j))
    nxt = lambda j: (lambda b, c: (b, jnp.minimum((c + 1) * per, n_tiles * per - 1), j))
    const = lambda b, c: (0, 0)
    return pl.pallas_call(
        functools.partial(_conf_kernel, n_tiles=n_tiles),
        grid=(bsz, n_tiles),
        in_specs=[
            pl.BlockSpec((1, TOK_TILE, ch), tile(col_v)), pl.BlockSpec((1, TOK_TILE, ch), tile(col_g)),
            pl.BlockSpec((1, CONF_HALO, ch), prev(col_v)), pl.BlockSpec((1, CONF_HALO, ch), prev(col_g)),
            pl.BlockSpec((1, CONF_HALO, ch), nxt(col_v)), pl.BlockSpec((1, CONF_HALO, ch), nxt(col_g)),
            pl.BlockSpec(conv_w.shape, const), pl.BlockSpec(conv_b.shape, const),
            pl.BlockSpec(ln_g.shape, const), pl.BlockSpec(ln_b.shape, const),
        ],
        out_specs=pl.BlockSpec((1, TOK_TILE, ch), lambda b, c: (b, c, 0)),
        out_shape=jax.ShapeDtypeStruct((bsz, ltot, ch), F32),
        scratch_shapes=[
            pltpu.VMEM((TOK_TILE + 2 * CONF_HALO, ch), F32),
            pltpu.VMEM((SUBLANES, TOK_TILE + (CONV_K - 1) // SUBLANES * SUBLANES, ch), F32),
        ],
        name="conformer_conv",
    )(u, u, u, u, u, u, conv_w, conv_b, ln_g, ln_b)


def _hgrn_kernel(q_ref, v_ref, z_ref, lb_ref, o_ref, st_ref, *, reverse):
    i = pl.program_id(0)
    bn = q_ref.shape[0]
    ch = GROUP_W
    n = bn * HGRN_CHUNK

    @pl.when(i == 0)
    def _():
        st_ref[...] = jnp.zeros_like(st_ref)

    def toward_keys(t, r):
        return pltpu.roll(t, (n - r) if reverse else r, 0)

    def toward_keys_in_block(t, r):
        t3 = t.reshape(n // SUBLANES, SUBLANES, ch)
        return pltpu.roll(t3, (SUBLANES - r) if reverse else r, 1).reshape(n, ch)

    def reaches(pos, r, size):
        return (pos < size - r) if reverse else (pos >= r)

    q = _silu(q_ref[...].reshape(n, ch))
    v = v_ref[...].reshape(n, ch)
    z = z_ref[...].reshape(n, ch)
    lb = lb_ref[...]
    f = jnp.maximum(lb, LB_FLOOR) + (1.0 - lb) * _sigmoid(z)
    g = jnp.log(f)
    k = 1.0 - f

    row = lax.broadcasted_iota(jnp.int32, (n, ch), 0)
    in_chunk = row & (HGRN_CHUNK - 1)
    gc = g
    sft = 1
    while sft < HGRN_CHUNK:
        gc = gc + jnp.where(reaches(in_chunk, sft, HGRN_CHUNK), toward_keys(gc, sft), 0.0)
        sft *= 2

    gc3 = gc.reshape(bn, HGRN_CHUNK, ch)
    last = 0 if reverse else HGRN_CHUNK - 1
    g_end = gc3[:, last:last + 1, :]
    q_state = (q * jnp.exp(gc)).astype(BF16)
    k_state = (k.reshape(bn, HGRN_CHUNK, ch) * jnp.exp(g_end - gc3)).reshape(n, ch).astype(BF16)
    v16 = v.astype(BF16)

    head_sum = _head_mean_matrix(HEAD_W, BF16)
    in_blk = row & (HGRN_EXACT - 1)
    out = jnp.dot((q * k).astype(BF16), head_sum, preferred_element_type=F32) * v
    for r in range(1, HGRN_EXACT):
        decay = jnp.exp(jnp.minimum(gc - toward_keys_in_block(gc, r), 0.0))
        w = jnp.where(reaches(in_blk, r, HGRN_EXACT), q * toward_keys_in_block(k, r) * decay, 0.0)
        out = out + jnp.dot(w.astype(BF16), head_sum, preferred_element_type=F32) * toward_keys_in_block(v, r)

    lane_head = _div_pow2(lax.broadcasted_iota(jnp.int32, (HGRN_CHUNK, ch), 1), HEAD_W)
    qi =lax.broadcasted_iota(jnp.int32, (HGRN_CHUNK, ch), 0)
    kj = lax.broadcasted_iota(jnp.int32, (HGRN_CHUNK, ch), 1) & (HEAD_W - 1)
    levels = []
    s = HGRN_CHUNK // 2
    while s >= HGRN_EXACT:
        g4 = gc.reshape(n // (2 * s), 2 * s, ch)
        edge = s if reverse else s - 1
        anchor = jnp.broadcast_to(g4[:, edge:edge + 1, :], g4.shape).reshape(n, ch)
        upper = (row & (2 * s - 1)) >= s
        is_query = jnp.logical_not(upper) if reverse else upper
        qs = jnp.where(is_query, q * jnp.exp(jnp.minimum(gc - anchor, 0.0)), 0.0).astype(BF16)
        ks = jnp.where(is_query, 0.0, k * jnp.exp(jnp.minimum(anchor - gc, 0.0))).astype(BF16)
        same_parent = _div_pow2(qi, 2 * s) == _div_pow2(kj, 2 * s)
        levels.append((qs, ks, same_parent))
        s //= 2

    def head_stack(t):
        return jnp.concatenate([jnp.where(lane_head == h, t, jnp.zeros_like(t)) for h in range(ch // HEAD_W)], axis=0)

    nt = (((1,), (1,)), ((), ()))
    tn = (((0,), (0,)), ((), ()))
    bd = _head_mean_matrix(HEAD_W, F32) > 0.5
    outs = []
    for b in range(bn):
        sl = slice(b * HGRN_CHUNK, (b + 1) * HGRN_CHUNK)
        scores = jnp.zeros((HGRN_CHUNK, ch), F32)
        for qs, ks, same_parent in levels:
            sc = lax.dot_general(qs[sl], head_stack(ks[sl]), nt, preferred_element_type=F32)
            scores = scores + jnp.where(same_parent, sc, 0.0)
        o_b = jnp.dot(scores.astype(BF16), head_stack(v16[sl]), preferred_element_type=F32)
        st = st_ref[b]
        o_b = o_b + lax.dot_general(q_state[sl], st.astype(BF16), nt, preferred_element_type=F32)
        upd = lax.dot_general(v16[sl], k_state[sl], tn, preferred_element_type=F32)
        st_ref[b] = st * jnp.exp(g_end[b]) + jnp.where(bd, upd, 0.0)
        outs.append(o_b)
    o_ref[...] = (out + jnp.concatenate(outs, axis=0)).reshape(bn, HGRN_CHUNK, ch)


def _hgrn(u, lb, reverse):
    bn, ltot, _ = u.shape
    n_chunks = ltot // HGRN_CHUNK
    n_ctx = CTX_LEN // HGRN_CHUNK

    def chunk(i):
        if not reverse:
            return i
        return jnp.where(i < n_ctx, n_ctx - 1 - i, n_chunks - 1 + n_ctx - i)

    col = lambda j: (lambda i: (0, chunk(i), j))
    return pl.pallas_call(
        functools.partial(_hgrn_kernel, reverse=reverse),
        grid=(n_chunks,),
        in_specs=[
            pl.BlockSpec((bn, HGRN_CHUNK, GROUP_W), col(2)),
            pl.BlockSpec((bn, HGRN_CHUNK, GROUP_W), col(3)),
            pl.BlockSpec((bn, HGRN_CHUNK, GROUP_W), col(5 if reverse else 4)),
            pl.BlockSpec((1, GROUP_W), lambda i: (0, 0)),
        ],
        out_specs=pl.BlockSpec((bn, HGRN_CHUNK, GROUP_W), lambda i: (0, chunk(i), 0)),
        out_shape=jax.ShapeDtypeStruct((bn, ltot, GROUP_W), F32),
        scratch_shapes=[pltpu.VMEM((bn, GROUP_W, GROUP_W), F32)],
        name="hgrn2_bwd" if reverse else "hgrn2_fwd",
    )(u, u, u, lb)


def _attn_kernel(lamp_ref, g_ref, qt_ref, k_ref, vt_ref, o_ref, s_scr, *, lam_init, q_start, nk):
    tq = TOK_TILE
    n_sub = o_ref.shape[1] // tq

    lp = lamp_ref[...]
    lam = (jnp.exp(jnp.sum(lp[0:1] * lp[1:2], axis=-1, keepdims=True))
           - jnp.exp(jnp.sum(lp[2:3] * lp[3:4], axis=-1, keepdims=True)) + lam_init)
    sub = _div_pow2(lax.broadcasted_iota(jnp.int32, (GROUP_W, tq), 0), SUB_HEAD_W)

    def scores(i, j):
        qt = qt_ref[0, :, pl.ds(pl.multiple_of(q_start + i * tq, tq), tq)]
        w = jnp.concatenate([jnp.where(sub == 2 * j, qt, jnp.zeros_like(qt)),
                             jnp.where(sub == 2 * j + 1, qt, jnp.zeros_like(qt))], axis=1)
        s = jnp.dot(k_ref[0, :nk, :], w, preferred_element_type=F32)
        s_scr[j % 2, :nk, :] = s
        return jnp.max(s, axis=0, keepdims=True)

    ones_rows = jnp.ones((ATT_SUM_ROWS, nk), BF16)

    def mix(j, m):
        p = jnp.exp2(s_scr[j % 2, :nk, :] - m).astype(BF16)
        lhs = jnp.concatenate([vt_ref[0, j * HEAD_W:(j + 1) * HEAD_W, :nk], ones_rows], axis=0)
        r = jnp.dot(lhs, p, preferred_element_type=F32)
        o, l = r[:HEAD_W], r[HEAD_W:HEAD_W + 1]
        return o[:, :tq] / l[:, :tq] - lam * (o[:, tq:] / l[:, tq:])

    def tile(i, m):
        heads = []
        for j in range(N_PAIRS):
            if j + 1 < N_PAIRS:
                m_next = scores(i, j + 1)
            else:
                m_next = scores(jnp.minimum(i + 1, n_sub - 1), 0)
            heads.append(mix(j, m))
            m = m_next
        ot = jnp.stack(heads)
        ms = jnp.mean(ot * ot, axis=1, keepdims=True)
        y = ot * lax.rsqrt(ms + RMS_EPS) * g_ref[...].reshape(1, HEAD_W, 1) * (1.0 - lam_init)
        o_ref[0, pl.ds(pl.multiple_of(i * tq, tq), tq), :] = y.reshape(GROUP_W, tq).T
        return m

    lax.fori_loop(0, n_sub, tile, scores(0, 0))


def _attention(lam_params, norm_g, q_t, k_rot, v_t, lam_init, q_start, n_q, nk):
    bsz, ltot, _ = k_rot.shape
    return pl.pallas_call(
        functools.partial(_attn_kernel, lam_init=lam_init, q_start=q_start, nk=nk),
        grid=(bsz,),
        in_specs=[
            pl.BlockSpec(lam_params.shape, lambda b: (0, 0)),
            pl.BlockSpec(norm_g.shape, lambda b: (0, 0)),
            pl.BlockSpec((1, GROUP_W, ltot), lambda b: (b, 0, 0)),
            pl.BlockSpec((1, ltot, GROUP_W), lambda b: (b, 0, 0)),
            pl.BlockSpec((1, GROUP_W, ltot), lambda b: (b, 0, 0)),
        ],
        out_specs=pl.BlockSpec((1, n_q, GROUP_W), lambda b: (b, 0, 0)),
        out_shape=jax.ShapeDtypeStruct((bsz, n_q, GROUP_W), F32),
        scratch_shapes=[pltpu.VMEM((2, nk, 2 * TOK_TILE), F32)],
        name="diff_attention",
    )(lam_params, norm_g, q_t, k_rot, v_t)


def _outproj_kernel(x_ref, mod_ref, gpost_ref, hgain_ref, w_ref, ga_ref, gb_ref, gc_ref, gd_ref,
                    ya_ref, ybf_ref, ybb_ref, yc_ref, yd_ref, ydc_ref, o_ref, *, ctx_row, t_off):
    b = pl.program_id(0)
    t = pl.program_id(1) + t_off
    row = jnp.where(t == 0, ctx_row, b)
    gate = mod_ref[pl.ds(row, 1), :][:, 2 * D_MODEL:]
    ya = ya_ref[0, 0] + ya_ref[1, 0]
    yb = _head_rms_norm(ybf_ref[0] + ybb_ref[0], hgain_ref[...])
    yd = jnp.where(t == 0, ydc_ref[0], yd_ref[0])
    groups = ((ya, ga_ref), (yb, gb_ref), (yc_ref[0], gc_ref), (yd, gd_ref))
    acc = jnp.zeros((x_ref.shape[1], D_MODEL), F32)
    for j, (y, g_ref) in enumerate(groups):
        m = (y * _silu(g_ref[0])).astype(BF16)
        acc = acc + jnp.dot(m, w_ref[j * GROUP_W:(j + 1) * GROUP_W, :], preferred_element_type=F32)
    r = acc * lax.rsqrt(jnp.mean(acc * acc, axis=-1, keepdims=True) + RMS_EPS) * gpost_ref[...]
    o_ref[0] = x_ref[0] + gate * r


def _outproj(xs, mod, g_post, hgrn_gain, w_out, u, ya2, yb_f, yb_b, yc, yd, yd_ctx, ctx_row, with_ctx):
    bsz, ltot, d = xs.shape
    t_off = 0 if with_ctx else CTX_LEN // TOK_TILE
    n_t = ltot // TOK_TILE - t_off
    col =lambda j: (lambda b, t: (b, t + t_off, j))
    tok = lambda b, t: (b, t + t_off, 0)
    two = lambda b, t: (0, b, t + t_off, 0)
    grp = pl.BlockSpec((1, TOK_TILE, GROUP_W), tok)
    return pl.pallas_call(
        functools.partial(_outproj_kernel, ctx_row=ctx_row, t_off=t_off),
        grid=(bsz, n_t),
        in_specs=[
            pl.BlockSpec((1, TOK_TILE, d), tok),
            pl.BlockSpec(mod.shape, lambda b, t: (0, 0)),
            pl.BlockSpec((1, d), lambda b, t: (0, 0)),
            pl.BlockSpec((1, GROUP_W), lambda b, t: (0, 0)),
            pl.BlockSpec(w_out.shape, lambda b, t: (0, 0)),
            pl.BlockSpec((1, TOK_TILE, GROUP_W), col(U_LRU_GATE)),
            pl.BlockSpec((1, TOK_TILE, GROUP_W), col(U_HGRN_GATE)),
            pl.BlockSpec((1, TOK_TILE, GROUP_W), col(U_CONF_GATE)),
            pl.BlockSpec((1, TOK_TILE, GROUP_W), col(U_DIFF_GATE)),
            pl.BlockSpec((2, 1, TOK_TILE, GROUP_W), two),
            grp, grp, grp,
            pl.BlockSpec((1, TOK_TILE, GROUP_W), lambda b, t: (b, jnp.maximum(t + t_off - 1, 0), 0)),
            pl.BlockSpec((1, TOK_TILE, GROUP_W), lambda b, t: (b, 0, 0)),
        ],
        out_specs=pl.BlockSpec((1, TOK_TILE, d), lambda b, t: (b, t, 0)),
        out_shape=jax.ShapeDtypeStruct((bsz, n_t * TOK_TILE, d), F32),
        name="out_projection",
    )(xs, mod, g_post, hgrn_gain, w_out, u, u, u, u, ya2, yb_f, yb_b, yc, yd, yd_ctx)


def _rope_tables(n_latent):
    rows = n_latent // GRID_W
    row = jnp.repeat(jnp.arange(rows, dtype=F32), GRID_W)
    colp = jnp.tile(jnp.arange(GRID_W, dtype=F32), rows)
    n_freq = SUB_HEAD_W // 4
    inv_freq = ROPE_THETA ** (-jnp.arange(n_freq, dtype=F32) / n_freq)
    ang = jnp.concatenate([row[:, None] * inv_freq, colp[:, None] * inv_freq], axis=-1)
    cos, sin = jnp.cos(ang), jnp.sin(ang)
    n_sub = GROUP_W // SUB_HEAD_W
    cos_t = jnp.tile(jnp.concatenate([cos, cos], axis=-1), (1, n_sub))
    sin_t = jnp.tile(jnp.concatenate([-sin, sin], axis=-1), (1, n_sub))
    cos_t = jnp.concatenate([jnp.ones((CTX_LEN, GROUP_W), F32), cos_t], axis=0)
    sin_t = jnp.concatenate([jnp.zeros((CTX_LEN, GROUP_W), F32), sin_t], axis=0)
    return cos_t, sin_t


def _block_diag(w):
    return jax.scipy.linalg.block_diag(*[w[h] for h in range(w.shape[0])])


def kernel(x, c, ctx, c_ctx, w_mod, b_mod, g_pre, g_post, w_in, w_out, lru_conv_w, lru_conv_b, lru_w_r, lru_b_r, lru_w_i, lru_b_i, lru_lambda, hgrn_lb, hgrn_norm_g, conf_conv_w, conf_conv_b, conf_ln_g, conf_ln_b, diff_lam_q1, diff_lam_k1, diff_lam_q2, diff_lam_k2, diff_norm_g):
    bsz, n_latent, d = x.shape
    depth = w_mod.shape[0]
    assert d == D_MODEL and ctx.shape[1] == CTX_LEN and n_latent % TOK_TILE == 0
    assert CTX_LEN == TOK_TILE

    ctx_row = bsz
    mod_rows = -(-(bsz + 1) // SUBLANES) * SUBLANES
    cc = jnp.zeros((mod_rows, d), F32).at[:bsz].set(c).at[ctx_row].set(c_ctx)
    mod = _modulation(cc, w_mod, b_mod[:, None, :])
    lbs = _lower_bounds(hgrn_lb.reshape(depth, 2 * GROUP_W)).reshape(depth, 2, 1, GROUP_W)
    cos_t, sin_t = _rope_tables(n_latent)

    xs = jnp.concatenate([ctx, x], axis=1)
    for layer in range(depth):
        with_ctx = layer < depth - 1
        w16 = w_in[layer].astype(BF16)
        w_main = jnp.concatenate([w16[:, :QKV_FIRST * GROUP_W], w16[:, (QKV_FIRST + 3) * GROUP_W:]], axis=1)
        w_qkv = w16[:, QKV_FIRST * GROUP_W:(QKV_FIRST + 3) * GROUP_W]
        u, q_t, k_rot, v_t = _inproj(xs, mod[layer], g_pre[layer][None], w_main, w_qkv, cos_t, sin_t, ctx_row)

        wg = jnp.stack([jnp.concatenate([_block_diag(lru_w_r[layer, dd]), _block_diag(lru_w_i[layer, dd])], axis=1)
                        for dd in range(2)]).astype(BF16)
        bg = jnp.concatenate([lru_b_r[layer], lru_b_i[layer]], axis=-1)[:, None, :]
        ya2 = _lru(u, lru_conv_w[layer], lru_conv_b[layer][None], wg, bg, lru_lambda[layer][:, None, :])

        yc = _conformer(u, conf_conv_w[layer], conf_conv_b[layer][None], conf_ln_g[layer][None],
                        conf_ln_b[layer][None])

        yb_f = _hgrn(u, lbs[layer, 0], reverse=False)
        yb_b = _hgrn(u, lbs[layer, 1], reverse=True)

        lam_init = 0.8 - 0.6 * math.exp(-0.3 * layer)
        lam_params = jnp.stack([diff_lam_q1[layer], diff_lam_k1[layer], diff_lam_q2[layer], diff_lam_k2[layer]])
        att = functools.partial(_attention, lam_params, diff_norm_g[layer][:, None], q_t, k_rot, v_t, lam_init)
        yd = att(CTX_LEN, n_latent, CTX_LEN + n_latent)
        yd_ctx = att(0, CTX_LEN, CTX_LEN) if with_ctx else yd

        xs = _outproj(xs, mod[layer], g_post[layer][None], jnp.tile(hgrn_norm_g[layer], N_PAIRS)[None],
                      w_out[layer].astype(BF16), u, ya2, yb_f, yb_b, yc, yd, yd_ctx, ctx_row, with_ctx)
    return xs
```

```python
import functools
import math

import jax
import jax.numpy as jnp
from jax import lax
from jax.experimental import pallas as pl
from jax.experimental.pallas import tpu as pltpu

F32 = jnp.float32
BF16 = jnp.bfloat16

D_MODEL = 1024
CTX_LEN = 256
GRID_W = 64
GROUP_W = 256
QKV_FIRST = 10
U_LRU_GATE, U_HGRN_GATE, U_CONF_GATE, U_DIFF_GATE = 1, 6, 9, 10
LRU_CONV = 4
LRU_C = 8.0
HEAD_W = 64
SUB_HEAD_W = 32
N_PAIRS = GROUP_W // HEAD_W
LB_FLOOR = 1e-30
CONV_K = 31
ROPE_THETA = 10000.0
RMS_EPS = 1e-6
LN_EPS = 1e-5
LOG2_E = math.log2(math.e)

SUBLANES = 8
BF16_SUBLANES = 16
MIX_DTYPE = BF16
TOK_TILE = 256
CONF_HALO = 16
HGRN_CHUNK = 64
HGRN_EXACT = 4
MOD_N_TILE = 1024
ATT_SUM_ROWS = 16


def _sigmoid(x):
    return 1.0 / (1.0 + jnp.exp(-x))


def _silu(x):
    return x * _sigmoid(x)


def _softplus(x):
    return jnp.maximum(x, 0.0) + jnp.log1p(jnp.exp(-jnp.abs(x)))


def _div_pow2(x, n):
    assert n & (n - 1) == 0
    return x >> (n.bit_length() - 1)


def _head_mean_matrix(width, dtype):
    r = _div_pow2(lax.broadcasted_iota(jnp.int32, (GROUP_W, GROUP_W), 0), width)
    c = _div_pow2(lax.broadcasted_iota(jnp.int32, (GROUP_W, GROUP_W), 1), width)
    return (r == c).astype(dtype)


def _head_rms_norm(o, gain):
    ms = jnp.dot(o * o, _head_mean_matrix(HEAD_W, F32), precision=lax.Precision.HIGHEST,
                 preferred_element_type=F32) * (1.0 / HEAD_W)
    return o * lax.rsqrt(ms + RMS_EPS) * gain


def _mod_kernel(cc_ref, w_ref, b_ref, o_ref):
    s = _silu(cc_ref[...])
    o_ref[0] = jnp.dot(s, w_ref[0], precision=lax.Precision.HIGHEST,
                       preferred_element_type=F32) + b_ref[0]


def _modulation(cc, w_mod, b_mod):
    depth, d, n = w_mod.shape
    rows = cc.shape[0]
    return pl.pallas_call(
        _mod_kernel,
        grid=(depth, n // MOD_N_TILE),
        in_specs=[
            pl.BlockSpec((rows, d), lambda l, j: (0, 0)),
            pl.BlockSpec((1, d, MOD_N_TILE), lambda l, j: (l, 0, j)),
            pl.BlockSpec((1, 1, MOD_N_TILE), lambda l, j: (l, 0, j)),
        ],
        out_specs=pl.BlockSpec((1, rows, MOD_N_TILE), lambda l, j: (l, 0, j)),
        out_shape=jax.ShapeDtypeStruct((depth, rows, n), F32),
        name="modulation",
    )(cc, w_mod, b_mod)


def _lb_kernel(x_ref, o_ref):
    x = x_ref[...]
    e = jnp.exp(x - jnp.max(x, axis=0, keepdims=True))
    p = e / jnp.sum(e, axis=0, keepdims=True)
    acc = jnp.zeros_like(p[0:1])
    for l in range(x.shape[0]):
        if l > 0:
            acc = acc + p[l:l + 1]
        o_ref[l:l + 1, :] = acc


def _lower_bounds(lb_raw):
    return pl.pallas_call(
        _lb_kernel, out_shape=jax.ShapeDtypeStruct(lb_raw.shape, F32), name="hgrn_lower_bounds",
    )(lb_raw)


def _inproj_kernel(x_ref, mod_ref, g_ref, w_ref, wqkv_ref, cos_ref, sin_ref, u_ref, qt_ref, ko_ref, vt_ref,
                   *, ctx_row):
    b = pl.program_id(0)
    t = pl.program_id(1)
    row = jnp.where(t == 0, ctx_row, b)
    mod = mod_ref[pl.ds(row, 1), :]
    shift = mod[:, :D_MODEL]
    scale = mod[:, D_MODEL:2 * D_MODEL]
    x = x_ref[0]
    y = x * lax.rsqrt(jnp.mean(x * x, axis=-1, keepdims=True) + RMS_EPS) * g_ref[...]
    h = y * (1.0 + scale) + shift
    h16 = h.astype(BF16)
    u_ref[0] = jnp.dot(h16, w_ref[...], preferred_element_type=F32)
    qkv = jnp.dot(h16, wqkv_ref[...], preferred_element_type=F32)

    cos = cos_ref[...]
    sin = sin_ref[...]
    lane = lax.broadcasted_iota(jnp.int32, cos.shape, 1)
    first_half = (lane & (SUB_HEAD_W - 1)) < SUB_HEAD_W // 2

    def rope(v):
        partner = jnp.where(first_half, pltpu.roll(v, GROUP_W - SUB_HEAD_W // 2, 1),
                            pltpu.roll(v, SUB_HEAD_W // 2, 1))
        return v * cos + partner * sin

    qt_ref[0] = (rope(qkv[:, :GROUP_W]) * (SUB_HEAD_W ** -0.5 * LOG2_E)).T.astype(BF16)
    ko_ref[0] = rope(qkv[:, GROUP_W:2 * GROUP_W]).astype(BF16)
    vt_ref[0] = qkv[:, 2 * GROUP_W:].T.astype(BF16)


def _inproj(xs, mod, g_pre, w_main, w_qkv, cos_t, sin_t, ctx_row):
    bsz, ltot, d = xs.shape
    n_main = w_main.shape[1]
    tok = pl.BlockSpec((1, TOK_TILE, GROUP_W), lambda b, t: (b, t, 0))
    tok_t = pl.BlockSpec((1, GROUP_W, TOK_TILE), lambda b, t: (b, 0, t))
    tab = pl.BlockSpec((TOK_TILE, GROUP_W), lambda b, t: (t, 0))
    return pl.pallas_call(
        functools.partial(_inproj_kernel, ctx_row=ctx_row),
        grid=(bsz, ltot // TOK_TILE),
        in_specs=[
            pl.BlockSpec((1, TOK_TILE, d), lambda b, t: (b, t, 0)),
            pl.BlockSpec(mod.shape, lambda b, t: (0, 0)),
            pl.BlockSpec((1, d), lambda b, t: (0, 0)),
            pl.BlockSpec(w_main.shape, lambda b, t: (0, 0)),
            pl.BlockSpec(w_qkv.shape, lambda b, t: (0, 0)),
            tab, tab,
        ],
        out_specs=[pl.BlockSpec((1, TOK_TILE, n_main), lambda b, t: (b, t, 0)), tok_t, tok, tok_t],
        out_shape=[
            jax.ShapeDtypeStruct((bsz, ltot, n_main), F32),
            jax.ShapeDtypeStruct((bsz, GROUP_W, ltot), BF16),
            jax.ShapeDtypeStruct((bsz, ltot, GROUP_W), BF16),
            jax.ShapeDtypeStruct((bsz, GROUP_W, ltot), BF16),
        ],
        name="in_projection",
    )(xs, mod, g_pre, w_main, w_qkv, cos_t, sin_t)


def _segment_halo_ok(c, n_tiles):
    return c >= 2, jnp.logical_and(c >= 1, c <= n_tiles - 2)


def _lru_kernel(xm_ref, xp_ref, xn_ref, cw_ref, cb_ref, wg_ref, bg_ref, lam_ref, o_ref, h_scr, *, n_tiles):
    d = pl.program_id(0)
    i = pl.program_id(1)
    c = _lru_tile(d, i, n_tiles)
    bsz, t_rows, ch = xm_ref.shape

    @pl.when(i == 0)
    def _():
        h_scr[...] = jnp.zeros_like(h_scr)

    left_ok, right_ok = _segment_halo_ok(c, n_tiles)
    ext = jnp.concatenate([jnp.where(left_ok, xp_ref[...], 0.0), xm_ref[...], jnp.where(right_ok, xn_ref[...], 0.0)],
                          axis=1)
    xc = jnp.zeros((bsz, t_rows, ch), F32) + cb_ref[...].reshape(1, 1, ch)
    for k in range(LRU_CONV):
        off = SUBLANES - LRU_CONV // 2 + k
        xc = xc + cw_ref[k:k + 1, :].reshape(1, 1, ch) * ext[:, off:off + t_rows]

    x2 = xc.reshape(bsz * t_rows, ch)
    gates = jnp.dot(x2.astype(BF16), wg_ref[0], preferred_element_type=F32) + bg_ref[0]
    r = _sigmoid(gates[:, :ch])
    ig = _sigmoid(gates[:, ch:])
    a = jnp.exp(-LRU_C * r * _softplus(-lam_ref[0]))
    uu = jnp.sqrt(jnp.maximum(1.0 - a * a, 0.0)) * (ig * x2)

    per = t_rows // SUBLANES
    pos = lax.broadcasted_iota(jnp.int32, (bsz * per, SUBLANES, ch), 1)

    def scan(reverse):
        av = a.reshape(bsz * per, SUBLANES, ch)
        uv = uu.reshape(bsz * per, SUBLANES, ch)
        sft = 1
        while sft < SUBLANES:
            ok = (pos < SUBLANES - sft) if reverse else (pos >= sft)
            roll_by = (SUBLANES - sft) if reverse else sft
            uv = jnp.where(ok, av * pltpu.roll(uv, roll_by, 1) + uv, uv)
            av = jnp.where(ok, av * pltpu.roll(av, roll_by, 1), av)
            sft *= 2
        carry = [h_scr[b] for b in range(bsz)]
        edge = 0 if reverse else SUBLANES - 1
        pair = BF16_SUBLANES // SUBLANES
        for w in (range(per // pair - 1, -1, -1) if reverse else range(per // pair)):
            for b in range(bsz):
                hs = {}
                for v in (range(pair * w + pair - 1, pair * w - 1, -1) if reverse else range(pair * w, pair * w + pair)):
                    hs[v] = uv[b * per + v] + av[b * per + v] * carry[b]
                    carry[b] = hs[v][edge:edge + 1]
                rows = jnp.concatenate([hs[v] for v in sorted(hs)], axis=0)
                o_ref[0, b, w * BF16_SUBLANES:(w + 1) * BF16_SUBLANES, :] = rows.astype(o_ref.dtype)
        for b in range(bsz):
            h_scr[b] = carry[b]

    @pl.when(d == 0)
    def _():
        scan(False)

    @pl.when(d == 1)
    def _():
        scan(True)


def _lru_tile(d, i, n_tiles):
    return jnp.where(d == 0, i, jnp.where(i == 0, 0, n_tiles - i))


def _lru(u, conv_w, conv_b, wg, bg, lam):
    bsz, ltot, _ = u.shape
    n_tiles = ltot // TOK_TILE
    per = TOK_TILE // SUBLANES
    ch = GROUP_W
    tile = lambda d, i: (0, _lru_tile(d, i, n_tiles), 0)
    prev = lambda d, i: (0, jnp.maximum(_lru_tile(d, i, n_tiles) * per - 1, 0), 0)
    nxt = lambda d, i: (0, jnp.minimum((_lru_tile(d, i, n_tiles) + 1) * per, n_tiles * per - 1), 0)
    return pl.pallas_call(
        functools.partial(_lru_kernel, n_tiles=n_tiles),
        grid=(2, n_tiles),
        in_specs=[
            pl.BlockSpec((bsz, TOK_TILE, ch), tile),
            pl.BlockSpec((bsz, SUBLANES, ch), prev),
            pl.BlockSpec((bsz, SUBLANES, ch), nxt),
            pl.BlockSpec(conv_w.shape, lambda d, i: (0, 0)),
            pl.BlockSpec(conv_b.shape, lambda d, i: (0, 0)),
            pl.BlockSpec((1,) + wg.shape[1:], lambda d, i: (d, 0, 0)),
            pl.BlockSpec((1,) + bg.shape[1:], lambda d, i: (d, 0, 0)),
            pl.BlockSpec((1,) + lam.shape[1:], lambda d, i: (d, 0, 0)),
        ],
        out_specs=pl.BlockSpec((1, bsz, TOK_TILE, ch), lambda d, i: (d, 0, _lru_tile(d, i, n_tiles), 0)),
        out_shape=jax.ShapeDtypeStruct((2, bsz, ltot, ch), MIX_DTYPE),
        scratch_shapes=[pltpu.VMEM((bsz, 1, ch), F32)],
        name="rglru",
    )(u, u, u, conv_w, conv_b, wg, bg, lam)


def _conf_kernel(vm_ref, gm_ref, vp_ref, gp_ref, vn_ref, gn_ref, cw_ref, cb_ref, lg_ref, lb_ref, o_ref, y_scr,
                 ys_scr, *, n_tiles):
    c = pl.program_id(1)
    t_rows, ch = vm_ref.shape[1], vm_ref.shape[2]
    left_ok, right_ok = _segment_halo_ok(c, n_tiles)

    def glu(v_ref, g_ref):
        return v_ref[0] * _sigmoid(g_ref[0])

    y_scr[0:CONF_HALO] = jnp.where(left_ok, glu(vp_ref, gp_ref), 0.0)
    y_scr[CONF_HALO:CONF_HALO + t_rows] = glu(vm_ref, gm_ref)
    y_scr[CONF_HALO + t_rows:] = jnp.where(right_ok, glu(vn_ref, gn_ref), 0.0)
    first = CONF_HALO - CONV_K // 2
    span = ys_scr.shape[1]
    acc = jnp.zeros((t_rows, ch), F32) + cb_ref[...]
    for j in range(SUBLANES):
        taps = [k for k in range(CONV_K) if (first + k) % SUBLANES == j]
        start = first + taps[0]
        ys_scr[j] = y_scr[start:start + span, :]
        for k in taps:
            lo = first + k - start
            acc = acc + cw_ref[k:k + 1, :] * ys_scr[j, lo:lo + t_rows, :]

    mu = jnp.mean(acc, axis=-1, keepdims=True)
    zc = acc - mu
    var = jnp.mean(zc * zc, axis=-1, keepdims=True)
    o_ref[0] = _silu(zc * lax.rsqrt(var + LN_EPS) * lg_ref[...] + lb_ref[...]).astype(o_ref.dtype)


def _conformer(u, conv_w, conv_b, ln_g, ln_b):
    bsz, ltot, _ = u.shape
    n_tiles = ltot // TOK_TILE
    per = TOK_TILE // CONF_HALO
    ch = GROUP_W
    col_v, col_g = 7, 8
    tile = lambda j: (lambda b, c: (b, c, j))
    prev = lambda j: (lambda b, c: (b, jnp.maximum(c * per - 1, 0), j))
    nxt = lambda j: (lambda b, c: (b, jnp.minimum((c + 1) * per, n_tiles * per - 1), j))
    const = lambda b, c: (0, 0)
    return pl.pallas_call(
        functools.partial(_conf_kernel, n_tiles=n_tiles),
        grid=(bsz, n_tiles),
        in_specs=[
            pl.BlockSpec((1, TOK_TILE, ch), tile(col_v)), pl.BlockSpec((1, TOK_TILE, ch), tile(col_g)),
            pl.BlockSpec((1, CONF_HALO, ch), prev(col_v)), pl.BlockSpec((1, CONF_HALO, ch), prev(col_g)),
            pl.BlockSpec((1, CONF_HALO, ch), nxt(col_v)), pl.BlockSpec((1, CONF_HALO, ch), nxt(col_g)),
            pl.BlockSpec(conv_w.shape, const), pl.BlockSpec(conv_b.shape, const),
            pl.BlockSpec(ln_g.shape, const), pl.BlockSpec(ln_b.shape, const),
        ],
        out_specs=pl.BlockSpec((1, TOK_TILE, ch), lambda b, c: (b, c, 0)),
        out_shape=jax.ShapeDtypeStruct((bsz, ltot, ch), MIX_DTYPE),
        scratch_shapes=[
            pltpu.VMEM((TOK_TILE + 2 * CONF_HALO, ch), F32),
            pltpu.VMEM((SUBLANES, TOK_TILE + (CONV_K - 1) // SUBLANES * SUBLANES, ch), F32),
        ],
        name="conformer_conv",
    )(u, u, u, u, u, u, conv_w, conv_b, ln_g, ln_b)


def _hgrn_kernel(q_ref, v_ref, z_ref, lb_ref, o_ref, st_ref, *, reverse):
    i = pl.program_id(0)
    bn = q_ref.shape[0]
    ch = GROUP_W
    n = bn * HGRN_CHUNK

    @pl.when(i == 0)
    def _():
        st_ref[...] = jnp.zeros_like(st_ref)

    def toward_keys(t, r):
        return pltpu.roll(t, (n - r) if reverse else r, 0)

    def toward_keys_in_block(t, r):
        t3 = t.reshape(n // SUBLANES, SUBLANES, ch)
        return pltpu.roll(t3, (SUBLANES - r) if reverse else r, 1).reshape(n, ch)

    def reaches(pos, r, size):
        return (pos < size - r) if reverse else (pos >= r)

    q = _silu(q_ref[...].reshape(n, ch))
    v = v_ref[...].reshape(n, ch)
    z = z_ref[...].reshape(n, ch)
    lb = lb_ref[...]
    f = jnp.maximum(lb, LB_FLOOR) + (1.0 - lb) * _sigmoid(z)
    g = jnp.log(f)
    k = 1.0 - f

    row = lax.broadcasted_iota(jnp.int32, (n, ch), 0)
    in_chunk = row & (HGRN_CHUNK - 1)
    gc = g
    sft = 1
    while sft < HGRN_CHUNK:
        gc = gc + jnp.where(reaches(in_chunk, sft, HGRN_CHUNK), toward_keys(gc, sft), 0.0)
        sft *= 2

    gc3 = gc.reshape(bn, HGRN_CHUNK, ch)
    last = 0 if reverse else HGRN_CHUNK - 1
    g_end = gc3[:, last:last + 1, :]
    q_state = (q * jnp.exp(gc)).astype(BF16)
    k_state = (k.reshape(bn, HGRN_CHUNK, ch) * jnp.exp(g_end - gc3)).reshape(n, ch).astype(BF16)
    v16 = v.astype(BF16)

    head_sum = _head_mean_matrix(HEAD_W, BF16)
    in_blk = row & (HGRN_EXACT - 1)
    out = jnp.dot((q * k).astype(BF16), head_sum, preferred_element_type=F32) * v
    for r in range(1, HGRN_EXACT):
        decay = jnp.exp(jnp.minimum(gc - toward_keys_in_block(gc, r), 0.0))
        w = jnp.where(reaches(in_blk, r, HGRN_EXACT), q * toward_keys_in_block(k, r) * decay, 0.0)
        out = out + jnp.dot(w.astype(BF16), head_sum, preferred_element_type=F32) * toward_keys_in_block(v, r)

    lane_head = _div_pow2(lax.broadcasted_iota(jnp.int32, (HGRN_CHUNK, ch), 1), HEAD_W)
    qi =lax.broadcasted_iota(jnp.int32, (HGRN_CHUNK, ch), 0)
    kj = lax.broadcasted_iota(jnp.int32, (HGRN_CHUNK, ch), 1) & (HEAD_W - 1)
    levels = []
    s = HGRN_CHUNK // 2
    while s >= HGRN_EXACT:
        g4 = gc.reshape(n // (2 * s), 2 * s, ch)
        edge = s if reverse else s - 1
        anchor = jnp.broadcast_to(g4[:, edge:edge + 1, :], g4.shape).reshape(n, ch)
        upper = (row & (2 * s - 1)) >= s
        is_query = jnp.logical_not(upper) if reverse else upper
        qs = jnp.where(is_query, q * jnp.exp(jnp.minimum(gc - anchor, 0.0)), 0.0).astype(BF16)
        ks = jnp.where(is_query, 0.0, k * jnp.exp(jnp.minimum(anchor - gc, 0.0))).astype(BF16)
        same_parent = _div_pow2(qi, 2 * s) == _div_pow2(kj, 2 * s)
        levels.append((qs, ks, same_parent))
        s //= 2

    def head_stack(t):
        return jnp.concatenate([jnp.where(lane_head == h, t, jnp.zeros_like(t)) for h in range(ch // HEAD_W)], axis=0)

    nt = (((1,), (1,)), ((), ()))
    tn = (((0,), (0,)), ((), ()))
    bd = _head_mean_matrix(HEAD_W, F32) > 0.5
    outs = []
    for b in range(bn):
        sl = slice(b * HGRN_CHUNK, (b + 1) * HGRN_CHUNK)
        scores = jnp.zeros((HGRN_CHUNK, ch), F32)
        for qs, ks, same_parent in levels:
            sc = lax.dot_general(qs[sl], head_stack(ks[sl]), nt, preferred_element_type=F32)
            scores = scores + jnp.where(same_parent, sc, 0.0)
        o_b = jnp.dot(scores.astype(BF16), head_stack(v16[sl]), preferred_element_type=F32)
        st = st_ref[b]
        o_b = o_b + lax.dot_general(q_state[sl], st.astype(BF16), nt, preferred_element_type=F32)
        upd = lax.dot_general(v16[sl], k_state[sl], tn, preferred_element_type=F32)
        st_ref[b] = st * jnp.exp(g_end[b]) + jnp.where(bd, upd, 0.0)
        outs.append(o_b)
    o_ref[...] = (out + jnp.concatenate(outs, axis=0)).reshape(bn, HGRN_CHUNK, ch).astype(o_ref.dtype)


def _hgrn(u, lb, reverse):
    bn, ltot, _ = u.shape
    n_chunks = ltot // HGRN_CHUNK
    n_ctx = CTX_LEN // HGRN_CHUNK

    def chunk(i):
        if not reverse:
            return i
        return jnp.where(i < n_ctx, n_ctx - 1 - i, n_chunks - 1 + n_ctx - i)

    col = lambda j: (lambda i: (0, chunk(i), j))
    return pl.pallas_call(
        functools.partial(_hgrn_kernel, reverse=reverse),
        grid=(n_chunks,),
        in_specs=[
            pl.BlockSpec((bn, HGRN_CHUNK, GROUP_W), col(2)),
            pl.BlockSpec((bn, HGRN_CHUNK, GROUP_W), col(3)),
            pl.BlockSpec((bn, HGRN_CHUNK, GROUP_W), col(5 if reverse else 4)),
            pl.BlockSpec((1, GROUP_W), lambda i: (0, 0)),
        ],
        out_specs=pl.BlockSpec((bn, HGRN_CHUNK, GROUP_W), lambda i: (0, chunk(i), 0)),
        out_shape=jax.ShapeDtypeStruct((bn, ltot, GROUP_W), MIX_DTYPE),
        scratch_shapes=[pltpu.VMEM((bn, GROUP_W, GROUP_W), F32)],
        name="hgrn2_bwd" if reverse else "hgrn2_fwd",
    )(u, u, u, lb)


def _attn_kernel(lamp_ref, g_ref, qt_ref, k_ref, vt_ref, o_ref, s_scr, *, lam_init, q_start, nk):
    tq = TOK_TILE
    n_sub = o_ref.shape[1] // tq

    lp = lamp_ref[...]
    lam = (jnp.exp(jnp.sum(lp[0:1] * lp[1:2], axis=-1, keepdims=True))
           - jnp.exp(jnp.sum(lp[2:3] * lp[3:4], axis=-1, keepdims=True)) + lam_init)
    sub = _div_pow2(lax.broadcasted_iota(jnp.int32, (GROUP_W, tq), 0), SUB_HEAD_W)

    def scores(i, j):
        qt = qt_ref[0, :, pl.ds(pl.multiple_of(q_start + i * tq, tq), tq)]
        w = jnp.concatenate([jnp.where(sub == 2 * j, qt, jnp.zeros_like(qt)),
                             jnp.where(sub == 2 * j + 1, qt, jnp.zeros_like(qt))], axis=1)
        s = jnp.dot(k_ref[0, :nk, :], w, preferred_element_type=F32)
        s_scr[j % 2, :nk, :] = s
        return jnp.max(s, axis=0, keepdims=True)

    ones_rows = jnp.ones((ATT_SUM_ROWS, nk), BF16)

    def mix(j, m):
        p = jnp.exp2(s_scr[j % 2, :nk, :] - m).astype(BF16)
        lhs = jnp.concatenate([vt_ref[0, j * HEAD_W:(j + 1) * HEAD_W, :nk], ones_rows], axis=0)
        r = jnp.dot(lhs, p, preferred_element_type=F32)
        o, l = r[:HEAD_W], r[HEAD_W:HEAD_W + 1]
        return o[:, :tq] / l[:, :tq] - lam * (o[:, tq:] / l[:, tq:])

    def tile(i, m):
        heads = []
        for j in range(N_PAIRS):
            if j + 1 < N_PAIRS:
                m_next = scores(i, j + 1)
            else:
                m_next = scores(jnp.minimum(i + 1, n_sub - 1), 0)
            heads.append(mix(j, m))
            m = m_next
        ot = jnp.stack(heads)
        ms = jnp.mean(ot * ot, axis=1, keepdims=True)
        y = ot * lax.rsqrt(ms + RMS_EPS) * g_ref[...].reshape(1, HEAD_W, 1) * (1.0 - lam_init)
        o_ref[0, pl.ds(pl.multiple_of(i * tq, tq), tq), :] = y.reshape(GROUP_W, tq).T.astype(o_ref.dtype)
        return m

    lax.fori_loop(0, n_sub, tile, scores(0, 0))


def _attention(lam_params, norm_g, q_t, k_rot, v_t, lam_init, q_start, n_q, nk):
    bsz, ltot, _ = k_rot.shape
    return pl.pallas_call(
        functools.partial(_attn_kernel, lam_init=lam_init, q_start=q_start, nk=nk),
        grid=(bsz,),
        in_specs=[
            pl.BlockSpec(lam_params.shape, lambda b: (0, 0)),
            pl.BlockSpec(norm_g.shape, lambda b: (0, 0)),
            pl.BlockSpec((1, GROUP_W, ltot), lambda b: (b, 0, 0)),
            pl.BlockSpec((1, ltot, GROUP_W), lambda b: (b, 0, 0)),
            pl.BlockSpec((1, GROUP_W, ltot), lambda b: (b, 0, 0)),
        ],
        out_specs=pl.BlockSpec((1, n_q, GROUP_W), lambda b: (b, 0, 0)),
        out_shape=jax.ShapeDtypeStruct((bsz, n_q, GROUP_W), MIX_DTYPE),
        scratch_shapes=[pltpu.VMEM((2, nk, 2 * TOK_TILE), F32)],
        name="diff_attention",
    )(lam_params, norm_g, q_t, k_rot, v_t)


def _outproj_kernel(x_ref, mod_ref, gpost_ref, hgain_ref, w_ref, ga_ref, gb_ref, gc_ref, gd_ref,
                    ya_ref, ybf_ref, ybb_ref, yc_ref, yd_ref, ydc_ref, o_ref, *, ctx_row, t_off):
    b = pl.program_id(0)
    t = pl.program_id(1) + t_off
    row = jnp.where(t == 0, ctx_row, b)
    gate = mod_ref[pl.ds(row, 1), :][:, 2 * D_MODEL:]
    ya = ya_ref[0, 0].astype(F32) + ya_ref[1, 0].astype(F32)
    yb = _head_rms_norm(ybf_ref[0].astype(F32) + ybb_ref[0].astype(F32), hgain_ref[...])
    yd = jnp.where(t == 0, ydc_ref[0], yd_ref[0]).astype(F32)
    groups = ((ya, ga_ref), (yb, gb_ref), (yc_ref[0].astype(F32), gc_ref), (yd, gd_ref))
    acc = jnp.zeros((x_ref.shape[1], D_MODEL), F32)
    for j, (y, g_ref) in enumerate(groups):
        m = (y * _silu(g_ref[0])).astype(BF16)
        acc = acc + jnp.dot(m, w_ref[j * GROUP_W:(j + 1) * GROUP_W, :], preferred_element_type=F32)
    r = acc * lax.rsqrt(jnp.mean(acc * acc, axis=-1, keepdims=True) + RMS_EPS) * gpost_ref[...]
    o_ref[0] = x_ref[0] + gate * r


def _outproj(xs, mod, g_post, hgrn_gain, w_out, u, ya2, yb_f, yb_b, yc, yd, yd_ctx, ctx_row, with_ctx):
    bsz, ltot, d = xs.shape
    t_off = 0 if with_ctx else CTX_LEN // TOK_TILE
    n_t = ltot // TOK_TILE - t_off
    col =lambda j: (lambda b, t: (b, t + t_off, j))
    tok = lambda b, t: (b, t + t_off, 0)
    two = lambda b, t: (0, b, t + t_off, 0)
    grp = pl.BlockSpec((1, TOK_TILE, GROUP_W), tok)
    return pl.pallas_call(
        functools.partial(_outproj_kernel, ctx_row=ctx_row, t_off=t_off),
        grid=(bsz, n_t),
        in_specs=[
            pl.BlockSpec((1, TOK_TILE, d), tok),
            pl.BlockSpec(mod.shape, lambda b, t: (0, 0)),
            pl.BlockSpec((1, d), lambda b, t: (0, 0)),
            pl.BlockSpec((1, GROUP_W), lambda b, t: (0, 0)),
            pl.BlockSpec(w_out.shape, lambda b, t: (0, 0)),
            pl.BlockSpec((1, TOK_TILE, GROUP_W), col(U_LRU_GATE)),
            pl.BlockSpec((1, TOK_TILE, GROUP_W), col(U_HGRN_GATE)),
            pl.BlockSpec((1, TOK_TILE, GROUP_W), col(U_CONF_GATE)),
            pl.BlockSpec((1, TOK_TILE, GROUP_W), col(U_DIFF_GATE)),
            pl.BlockSpec((2, 1, TOK_TILE, GROUP_W), two),
            grp, grp, grp,
            pl.BlockSpec((1, TOK_TILE, GROUP_W), lambda b, t: (b, jnp.maximum(t + t_off - 1, 0), 0)),
            pl.BlockSpec((1, TOK_TILE, GROUP_W), lambda b, t: (b, 0, 0)),
        ],
        out_specs=pl.BlockSpec((1, TOK_TILE, d), lambda b, t: (b, t, 0)),
        out_shape=jax.ShapeDtypeStruct((bsz, n_t * TOK_TILE, d), F32),
        name="out_projection",
    )(xs, mod, g_post, hgrn_gain, w_out, u, u, u, u, ya2, yb_f, yb_b, yc, yd, yd_ctx)


def _rope_tables(n_latent):
    rows = n_latent // GRID_W
    row = jnp.repeat(jnp.arange(rows, dtype=F32), GRID_W)
    colp = jnp.tile(jnp.arange(GRID_W, dtype=F32), rows)
    n_freq = SUB_HEAD_W // 4
    inv_freq = ROPE_THETA ** (-jnp.arange(n_freq, dtype=F32) / n_freq)
    ang = jnp.concatenate([row[:, None] * inv_freq, colp[:, None] * inv_freq], axis=-1)
    cos, sin = jnp.cos(ang), jnp.sin(ang)
    n_sub = GROUP_W // SUB_HEAD_W
    cos_t = jnp.tile(jnp.concatenate([cos, cos], axis=-1), (1, n_sub))
    sin_t = jnp.tile(jnp.concatenate([-sin, sin], axis=-1), (1, n_sub))
    cos_t = jnp.concatenate([jnp.ones((CTX_LEN, GROUP_W), F32), cos_t], axis=0)
    sin_t = jnp.concatenate([jnp.zeros((CTX_LEN, GROUP_W), F32), sin_t], axis=0)
    return cos_t, sin_t


def _block_diag(w):
    return jax.scipy.linalg.block_diag(*[w[h] for h in range(w.shape[0])])


def kernel(x, c, ctx, c_ctx, w_mod, b_mod, g_pre, g_post, w_in, w_out, lru_conv_w, lru_conv_b, lru_w_r, lru_b_r, lru_w_i, lru_b_i, lru_lambda, hgrn_lb, hgrn_norm_g, conf_conv_w, conf_conv_b, conf_ln_g, conf_ln_b, diff_lam_q1, diff_lam_k1, diff_lam_q2, diff_lam_k2, diff_norm_g):
    bsz, n_latent, d = x.shape
    depth = w_mod.shape[0]
    assert d == D_MODEL and ctx.shape[1] == CTX_LEN and n_latent % TOK_TILE == 0
    assert CTX_LEN == TOK_TILE

    ctx_row = bsz
    mod_rows = -(-(bsz + 1) // SUBLANES) * SUBLANES
    cc = jnp.zeros((mod_rows, d), F32).at[:bsz].set(c).at[ctx_row].set(c_ctx)
    mod = _modulation(cc, w_mod, b_mod[:, None, :])
    lbs = _lower_bounds(hgrn_lb.reshape(depth, 2 * GROUP_W)).reshape(depth, 2, 1, GROUP_W)
    cos_t, sin_t = _rope_tables(n_latent)

    xs = jnp.concatenate([ctx, x], axis=1)
    for layer in range(depth):
        with_ctx = layer < depth - 1
        w16 = w_in[layer].astype(BF16)
        w_main = jnp.concatenate([w16[:, :QKV_FIRST * GROUP_W], w16[:, (QKV_FIRST + 3) * GROUP_W:]], axis=1)
        w_qkv = w16[:, QKV_FIRST * GROUP_W:(QKV_FIRST + 3) * GROUP_W]
        u, q_t, k_rot, v_t = _inproj(xs, mod[layer], g_pre[layer][None], w_main, w_qkv, cos_t, sin_t, ctx_row)

        wg = jnp.stack([jnp.concatenate([_block_diag(lru_w_r[layer, dd]), _block_diag(lru_w_i[layer, dd])], axis=1)
                        for dd in range(2)]).astype(BF16)
        bg = jnp.concatenate([lru_b_r[layer], lru_b_i[layer]], axis=-1)[:, None, :]
        ya2 = _lru(u, lru_conv_w[layer], lru_conv_b[layer][None], wg, bg, lru_lambda[layer][:, None, :])

        yc = _conformer(u, conf_conv_w[layer], conf_conv_b[layer][None], conf_ln_g[layer][None],
                        conf_ln_b[layer][None])

        yb_f = _hgrn(u, lbs[layer, 0], reverse=False)
        yb_b = _hgrn(u, lbs[layer, 1], reverse=True)

        lam_init = 0.8 - 0.6 * math.exp(-0.3 * layer)
        lam_params = jnp.stack([diff_lam_q1[layer], diff_lam_k1[layer], diff_lam_q2[layer], diff_lam_k2[layer]])
        att = functools.partial(_attention, lam_params, diff_norm_g[layer][:, None], q_t, k_rot, v_t, lam_init)
        yd = att(CTX_LEN, n_latent, CTX_LEN + n_latent)
        yd_ctx = att(0, CTX_LEN, CTX_LEN) if with_ctx else yd

        xs = _outproj(xs, mod[layer], g_post[layer][None], jnp.tile(hgrn_norm_g[layer], N_PAIRS)[None],
                      w_out[layer].astype(BF16), u, ya2, yb_f, yb_b, yc, yd, yd_ctx, ctx_row, with_ctx)
    return xs
```

```python
import functools
import math

import jax
import jax.numpy as jnp
from jax import lax
from jax.experimental import pallas as pl
from jax.experimental.pallas import tpu as pltpu

F32 = jnp.float32
BF16 = jnp.bfloat16

D_MODEL = 1024
CTX_LEN = 256
GRID_W = 64
GROUP_W = 256
QKV_FIRST = 10
U_LRU_GATE, U_HGRN_GATE, U_CONF_GATE, U_DIFF_GATE = 1, 6, 9, 10
LRU_CONV = 4
LRU_C = 8.0
HEAD_W = 64
SUB_HEAD_W = 32
N_PAIRS = GROUP_W // HEAD_W
LB_FLOOR = 1e-30
CONV_K = 31
ROPE_THETA = 10000.0
RMS_EPS = 1e-6
LN_EPS = 1e-5
LOG2_E = math.log2(math.e)

SUBLANES = 8
BF16_SUBLANES = 16
MIX_DTYPE = BF16
TOK_TILE = 256
IN_SAMPLES = 2
CONF_HALO = 16
HGRN_CHUNK = 64
HGRN_EXACT = 4
MOD_N_TILE = 1024
ATT_SUM_ROWS = 16
ATT_KEY_PARTS = 8


def _sigmoid(x):
    return 1.0 / (1.0 + jnp.exp(-x))


def _silu(x):
    return x * _sigmoid(x)


def _softplus(x):
    return jnp.maximum(x, 0.0) + jnp.log1p(jnp.exp(-jnp.abs(x)))


def _div_pow2(x, n):
    assert n & (n - 1) == 0
    return x >> (n.bit_length() - 1)


def _head_mean_matrix(width, dtype):
    r = _div_pow2(lax.broadcasted_iota(jnp.int32, (GROUP_W, GROUP_W), 0), width)
    c = _div_pow2(lax.broadcasted_iota(jnp.int32, (GROUP_W, GROUP_W), 1), width)
    return (r == c).astype(dtype)


def _head_rms_norm(o, gain):
    ms = jnp.dot(o * o, _head_mean_matrix(HEAD_W, F32), precision=lax.Precision.HIGHEST,
                 preferred_element_type=F32) * (1.0 / HEAD_W)
    return o * lax.rsqrt(ms + RMS_EPS) * gain


def _mod_kernel(cc_ref, w_ref, b_ref, o_ref):
    s = _silu(cc_ref[...])
    o_ref[0] = jnp.dot(s, w_ref[0], precision=lax.Precision.HIGHEST,
                       preferred_element_type=F32) + b_ref[0]


def _modulation(cc, w_mod, b_mod):
    depth, d, n = w_mod.shape
    rows = cc.shape[0]
    return pl.pallas_call(
        _mod_kernel,
        grid=(depth, n // MOD_N_TILE),
        in_specs=[
            pl.BlockSpec((rows, d), lambda l, j: (0, 0)),
            pl.BlockSpec((1, d, MOD_N_TILE), lambda l, j: (l, 0, j)),
            pl.BlockSpec((1, 1, MOD_N_TILE), lambda l, j: (l, 0, j)),
        ],
        out_specs=pl.BlockSpec((1, rows, MOD_N_TILE), lambda l, j: (l, 0, j)),
        out_shape=jax.ShapeDtypeStruct((depth, rows, n), F32),
        name="modulation",
    )(cc, w_mod, b_mod)


def _lb_kernel(x_ref, o_ref):
    x = x_ref[...]
    e = jnp.exp(x - jnp.max(x, axis=0, keepdims=True))
    p = e / jnp.sum(e, axis=0, keepdims=True)
    acc = jnp.zeros_like(p[0:1])
    for l in range(x.shape[0]):
        if l > 0:
            acc = acc + p[l:l + 1]
        o_ref[l:l + 1, :] = acc


def _lower_bounds(lb_raw):
    return pl.pallas_call(
        _lb_kernel, out_shape=jax.ShapeDtypeStruct(lb_raw.shape, F32), name="hgrn_lower_bounds",
    )(lb_raw)


def _inproj_kernel(x_ref, mod_ref, g_ref, w_ref, wqkv_ref, cos_ref, sin_ref, u_ref, qt_ref, ko_ref, vt_ref,
                   *, ctx_row):
    b0 = pl.program_id(0) * IN_SAMPLES
    t = pl.program_id(1)
    rows = x_ref.shape[1]
    hs = []
    for s in range(IN_SAMPLES):
        row = jnp.where(t == 0, ctx_row, b0 + s)
        mod = mod_ref[pl.ds(row, 1), :]
        shift = mod[:, :D_MODEL]
        scale = mod[:, D_MODEL:2 * D_MODEL]
        x = x_ref[s]
        y = x * lax.rsqrt(jnp.mean(x * x, axis=-1, keepdims=True) + RMS_EPS) * g_ref[...]
        hs.append((y * (1.0 + scale) + shift).astype(BF16))
    h16 = jnp.concatenate(hs, axis=0)
    u_ref[...] = jnp.dot(h16, w_ref[...], preferred_element_type=F32).reshape(u_ref.shape)
    qkv = jnp.dot(h16, wqkv_ref[...], preferred_element_type=F32)

    cos = cos_ref[...]
    sin = sin_ref[...]
    lane = lax.broadcasted_iota(jnp.int32, cos.shape, 1)
    first_half = (lane & (SUB_HEAD_W - 1)) < SUB_HEAD_W // 2

    def rope(v):
        partner = jnp.where(first_half, pltpu.roll(v, GROUP_W - SUB_HEAD_W // 2, 1),
                            pltpu.roll(v, SUB_HEAD_W // 2, 1))
        return v * cos + partner * sin

    for s in range(IN_SAMPLES):
        one = qkv[s * rows:(s + 1) * rows]
        qt_ref[s] = (rope(one[:, :GROUP_W]) * (SUB_HEAD_W ** -0.5 * LOG2_E)).T.astype(BF16)
        ko_ref[s] = rope(one[:, GROUP_W:2 * GROUP_W]).astype(BF16)
        vt_ref[s] = one[:, 2 * GROUP_W:].T.astype(BF16)


def _inproj(xs, mod, g_pre, w_main, w_qkv, cos_t, sin_t, ctx_row):
    bsz, ltot, d = xs.shape
    assert bsz % IN_SAMPLES == 0
    n_main = w_main.shape[1]
    tok = pl.BlockSpec((IN_SAMPLES, TOK_TILE, GROUP_W), lambda b, t: (b, t, 0))
    tok_t = pl.BlockSpec((IN_SAMPLES, GROUP_W, TOK_TILE), lambda b, t: (b, 0, t))
    tab = pl.BlockSpec((TOK_TILE, GROUP_W), lambda b, t: (t, 0))
    return pl.pallas_call(
        functools.partial(_inproj_kernel, ctx_row=ctx_row),
        grid=(bsz // IN_SAMPLES, ltot // TOK_TILE),
        in_specs=[
            pl.BlockSpec((IN_SAMPLES, TOK_TILE, d), lambda b, t: (b, t, 0)),
            pl.BlockSpec(mod.shape, lambda b, t: (0, 0)),
            pl.BlockSpec((1, d), lambda b, t: (0, 0)),
            pl.BlockSpec(w_main.shape, lambda b, t: (0, 0)),
            pl.BlockSpec(w_qkv.shape, lambda b, t: (0, 0)),
            tab, tab,
        ],
        out_specs=[pl.BlockSpec((IN_SAMPLES, TOK_TILE, n_main), lambda b, t: (b, t, 0)), tok_t, tok, tok_t],
        out_shape=[
            jax.ShapeDtypeStruct((bsz, ltot, n_main), F32),
            jax.ShapeDtypeStruct((bsz, GROUP_W, ltot), BF16),
            jax.ShapeDtypeStruct((bsz, ltot, GROUP_W), BF16),
            jax.ShapeDtypeStruct((bsz, GROUP_W, ltot), BF16),
        ],
        name="in_projection",
    )(xs, mod, g_pre, w_main, w_qkv, cos_t, sin_t)


def _segment_halo_ok(c, n_tiles):
    return c >= 2, jnp.logical_and(c >= 1, c <= n_tiles - 2)


def _lru_kernel(xm_ref, xp_ref, xn_ref, cw_ref, cb_ref, wg_ref, bg_ref, lam_ref, o_ref, h_scr, *, n_tiles):
    d = pl.program_id(0)
    i = pl.program_id(1)
    c = _lru_tile(d, i, n_tiles)
    bsz, t_rows, ch = xm_ref.shape

    @pl.when(i == 0)
    def _():
        h_scr[...] = jnp.zeros_like(h_scr)

    left_ok, right_ok = _segment_halo_ok(c, n_tiles)
    ext = jnp.concatenate([jnp.where(left_ok, xp_ref[...], 0.0), xm_ref[...], jnp.where(right_ok, xn_ref[...], 0.0)],
                          axis=1)
    xc = jnp.zeros((bsz, t_rows, ch), F32) + cb_ref[...].reshape(1, 1, ch)
    for k in range(LRU_CONV):
        off = SUBLANES - LRU_CONV // 2 + k
        xc = xc + cw_ref[k:k + 1, :].reshape(1, 1, ch) * ext[:, off:off + t_rows]

    x2 = xc.reshape(bsz * t_rows, ch)
    gates = jnp.dot(x2.astype(BF16), wg_ref[0], preferred_element_type=F32) + bg_ref[0]
    r = _sigmoid(gates[:, :ch])
    ig = _sigmoid(gates[:, ch:])
    a = jnp.exp(-LRU_C * r * _softplus(-lam_ref[0]))
    uu = jnp.sqrt(jnp.maximum(1.0 - a * a, 0.0)) * (ig * x2)

    per = t_rows // SUBLANES
    pos = lax.broadcasted_iota(jnp.int32, (bsz * per, SUBLANES, ch), 1)

    def scan(reverse):
        av = a.reshape(bsz * per, SUBLANES, ch)
        uv = uu.reshape(bsz * per, SUBLANES, ch)
        sft = 1
        while sft < SUBLANES:
            ok = (pos < SUBLANES - sft) if reverse else (pos >= sft)
            roll_by = (SUBLANES - sft) if reverse else sft
            uv = jnp.where(ok, av * pltpu.roll(uv, roll_by, 1) + uv, uv)
            av = jnp.where(ok, av * pltpu.roll(av, roll_by, 1), av)
            sft *= 2
        carry = [h_scr[b] for b in range(bsz)]
        edge = 0 if reverse else SUBLANES - 1
        pair = BF16_SUBLANES // SUBLANES
        for w in (range(per // pair - 1, -1, -1) if reverse else range(per // pair)):
            for b in range(bsz):
                hs = {}
                for v in (range(pair * w + pair - 1, pair * w - 1, -1) if reverse else range(pair * w, pair * w + pair)):
                    hs[v] = uv[b * per + v] + av[b * per + v] * carry[b]
                    carry[b] = hs[v][edge:edge + 1]
                rows = jnp.concatenate([hs[v] for v in sorted(hs)], axis=0)
                o_ref[0, b, w * BF16_SUBLANES:(w + 1) * BF16_SUBLANES, :] = rows.astype(o_ref.dtype)
        for b in range(bsz):
            h_scr[b] = carry[b]

    @pl.when(d == 0)
    def _():
        scan(False)

    @pl.when(d == 1)
    def _():
        scan(True)


def _lru_tile(d, i, n_tiles):
    return jnp.where(d == 0, i, jnp.where(i == 0, 0, n_tiles - i))


def _lru(u, conv_w, conv_b, wg, bg, lam):
    bsz, ltot, _ = u.shape
    n_tiles = ltot // TOK_TILE
    per = TOK_TILE // SUBLANES
    ch = GROUP_W
    tile = lambda d, i: (0, _lru_tile(d, i, n_tiles), 0)
    prev = lambda d, i: (0, jnp.maximum(_lru_tile(d, i, n_tiles) * per - 1, 0), 0)
    nxt = lambda d, i: (0, jnp.minimum((_lru_tile(d, i, n_tiles) + 1) * per, n_tiles * per - 1), 0)
    return pl.pallas_call(
        functools.partial(_lru_kernel, n_tiles=n_tiles),
        grid=(2, n_tiles),
        in_specs=[
            pl.BlockSpec((bsz, TOK_TILE, ch), tile),
            pl.BlockSpec((bsz, SUBLANES, ch), prev),
            pl.BlockSpec((bsz, SUBLANES, ch), nxt),
            pl.BlockSpec(conv_w.shape, lambda d, i: (0, 0)),
            pl.BlockSpec(conv_b.shape, lambda d, i: (0, 0)),
            pl.BlockSpec((1,) + wg.shape[1:], lambda d, i: (d, 0, 0)),
            pl.BlockSpec((1,) + bg.shape[1:], lambda d, i: (d, 0, 0)),
            pl.BlockSpec((1,) + lam.shape[1:], lambda d, i: (d, 0, 0)),
        ],
        out_specs=pl.BlockSpec((1, bsz, TOK_TILE, ch), lambda d, i: (d, 0, _lru_tile(d, i, n_tiles), 0)),
        out_shape=jax.ShapeDtypeStruct((2, bsz, ltot, ch), MIX_DTYPE),
        scratch_shapes=[pltpu.VMEM((bsz, 1, ch), F32)],
        name="rglru",
    )(u, u, u, conv_w, conv_b, wg, bg, lam)


def _conf_kernel(vm_ref, gm_ref, vp_ref, gp_ref, vn_ref, gn_ref, cw_ref, cb_ref, lg_ref, lb_ref, o_ref, y_scr,
                 ys_scr, *, n_tiles):
    c = pl.program_id(1)
    t_rows, ch = vm_ref.shape[1], vm_ref.shape[2]
    left_ok, right_ok = _segment_halo_ok(c, n_tiles)

    def glu(v_ref, g_ref):
        return v_ref[0] * _sigmoid(g_ref[0])

    y_scr[0:CONF_HALO] = jnp.where(left_ok, glu(vp_ref, gp_ref), 0.0)
    y_scr[CONF_HALO:CONF_HALO + t_rows] = glu(vm_ref, gm_ref)
    y_scr[CONF_HALO + t_rows:] = jnp.where(right_ok, glu(vn_ref, gn_ref), 0.0)
    first = CONF_HALO - CONV_K // 2
    span = ys_scr.shape[1]
    acc = jnp.zeros((t_rows, ch), F32) + cb_ref[...]
    for j in range(SUBLANES):
        taps = [k for k in range(CONV_K) if (first + k) % SUBLANES == j]
        start = first + taps[0]
        ys_scr[j] = y_scr[start:start + span, :]
        for k in taps:
            lo = first + k - start
            acc = acc + cw_ref[k:k + 1, :] * ys_scr[j, lo:lo + t_rows, :]

    mu = jnp.mean(acc, axis=-1, keepdims=True)
    zc = acc - mu
    var = jnp.mean(zc * zc, axis=-1, keepdims=True)
    o_ref[0] = _silu(zc * lax.rsqrt(var + LN_EPS) * lg_ref[...] + lb_ref[...]).astype(o_ref.dtype)


def _conformer(u, conv_w, conv_b, ln_g, ln_b):
    bsz, ltot, _ = u.shape
    n_tiles = ltot // TOK_TILE
    per = TOK_TILE // CONF_HALO
    ch = GROUP_W
    col_v, col_g = 7, 8
    tile = lambda j: (lambda b, c: (b, c, j))
    prev = lambda j: (lambda b, c: (b, jnp.maximum(c * per - 1, 0), j))
    nxt = lambda j: (lambda b, c: (b, jnp.minimum((c + 1) * per, n_tiles * per - 1), j))
    const = lambda b, c: (0, 0)
    return pl.pallas_call(
        functools.partial(_conf_kernel, n_tiles=n_tiles),
        grid=(bsz, n_tiles),
        in_specs=[
            pl.BlockSpec((1, TOK_TILE, ch), tile(col_v)), pl.BlockSpec((1, TOK_TILE, ch), tile(col_g)),
            pl.BlockSpec((1, CONF_HALO, ch), prev(col_v)), pl.BlockSpec((1, CONF_HALO, ch), prev(col_g)),
            pl.BlockSpec((1, CONF_HALO, ch), nxt(col_v)), pl.BlockSpec((1, CONF_HALO, ch), nxt(col_g)),
            pl.BlockSpec(conv_w.shape, const), pl.BlockSpec(conv_b.shape, const),
            pl.BlockSpec(ln_g.shape, const), pl.BlockSpec(ln_b.shape, const),
        ],
        out_specs=pl.BlockSpec((1, TOK_TILE, ch), lambda b, c: (b, c, 0)),
        out_shape=jax.ShapeDtypeStruct((bsz, ltot, ch), MIX_DTYPE),
        scratch_shapes=[
            pltpu.VMEM((TOK_TILE + 2 * CONF_HALO, ch), F32),
            pltpu.VMEM((SUBLANES, TOK_TILE + (CONV_K - 1) // SUBLANES * SUBLANES, ch), F32),
        ],
        name="conformer_conv",
    )(u, u, u, u, u, u, conv_w, conv_b, ln_g, ln_b)


def _hgrn_kernel(q_ref, v_ref, z_ref, lb_ref, o_ref, st_ref, *, reverse):
    i = pl.program_id(0)
    bn = q_ref.shape[0]
    ch = GROUP_W
    n = bn * HGRN_CHUNK

    @pl.when(i == 0)
    def _():
        st_ref[...] = jnp.zeros_like(st_ref)

    def toward_keys(t, r):
        return pltpu.roll(t, (n - r) if reverse else r, 0)

    def toward_keys_in_block(t, r):
        t3 = t.reshape(n // SUBLANES, SUBLANES, ch)
        return pltpu.roll(t3, (SUBLANES - r) if reverse else r, 1).reshape(n, ch)

    def reaches(pos, r, size):
        return (pos < size - r) if reverse else (pos >= r)

    q = _silu(q_ref[...].reshape(n, ch))
    v = v_ref[...].reshape(n, ch)
    z = z_ref[...].reshape(n, ch)
    lb = lb_ref[...]
    f = jnp.maximum(lb, LB_FLOOR) + (1.0 - lb) * _sigmoid(z)
    g = jnp.log(f)
    k = 1.0 - f

    row = lax.broadcasted_iota(jnp.int32, (n, ch), 0)
    in_chunk = row & (HGRN_CHUNK - 1)
    gc = g
    sft = 1
    while sft < HGRN_CHUNK:
        gc = gc + jnp.where(reaches(in_chunk, sft, HGRN_CHUNK), toward_keys(gc, sft), 0.0)
        sft *= 2

    gc3 = gc.reshape(bn, HGRN_CHUNK, ch)
    last = 0 if reverse else HGRN_CHUNK - 1
    g_end = gc3[:, last:last + 1, :]
    q_state = (q * jnp.exp(gc)).astype(BF16)
    k_state = (k.reshape(bn, HGRN_CHUNK, ch) * jnp.exp(g_end - gc3)).reshape(n, ch).astype(BF16)
    v16 = v.astype(BF16)

    head_sum = _head_mean_matrix(HEAD_W, BF16)
    in_blk = row & (HGRN_EXACT - 1)
    out = jnp.dot((q * k).astype(BF16), head_sum, preferred_element_type=F32) * v
    for r in range(1, HGRN_EXACT):
        decay = jnp.exp(jnp.minimum(gc - toward_keys_in_block(gc, r), 0.0))
        w = jnp.where(reaches(in_blk, r, HGRN_EXACT), q * toward_keys_in_block(k, r) * decay, 0.0)
        out = out + jnp.dot(w.astype(BF16), head_sum, preferred_element_type=F32) * toward_keys_in_block(v, r)

    lane_head = _div_pow2(lax.broadcasted_iota(jnp.int32, (HGRN_CHUNK, ch), 1), HEAD_W)
    qi =lax.broadcasted_iota(jnp.int32, (HGRN_CHUNK, ch), 0)
    kj = lax.broadcasted_iota(jnp.int32, (HGRN_CHUNK, ch), 1) & (HEAD_W - 1)
    levels = []
    s = HGRN_CHUNK // 2
    while s >= HGRN_EXACT:
        g4 = gc.reshape(n // (2 * s), 2 * s, ch)
        edge = s if reverse else s - 1
        anchor = jnp.broadcast_to(g4[:, edge:edge + 1, :], g4.shape).reshape(n, ch)
        upper = (row & (2 * s - 1)) >= s
        is_query = jnp.logical_not(upper) if reverse else upper
        qs = jnp.where(is_query, q * jnp.exp(jnp.minimum(gc - anchor, 0.0)), 0.0).astype(BF16)
        ks = jnp.where(is_query, 0.0, k * jnp.exp(jnp.minimum(anchor - gc, 0.0))).astype(BF16)
        same_parent = _div_pow2(qi, 2 * s) == _div_pow2(kj, 2 * s)
        levels.append((qs, ks, same_parent))
        s //= 2

    def head_stack(t):
        return jnp.concatenate([jnp.where(lane_head == h, t, jnp.zeros_like(t)) for h in range(ch // HEAD_W)], axis=0)

    nt = (((1,), (1,)), ((), ()))
    tn = (((0,), (0,)), ((), ()))
    bd = _head_mean_matrix(HEAD_W, F32) > 0.5
    outs = []
    for b in range(bn):
        sl = slice(b * HGRN_CHUNK, (b + 1) * HGRN_CHUNK)
        scores = jnp.zeros((HGRN_CHUNK, ch), F32)
        for qs, ks, same_parent in levels:
            sc = lax.dot_general(qs[sl], head_stack(ks[sl]), nt, preferred_element_type=F32)
            scores = scores + jnp.where(same_parent, sc, 0.0)
        o_b = jnp.dot(scores.astype(BF16), head_stack(v16[sl]), preferred_element_type=F32)
        st = st_ref[b]
        o_b = o_b + lax.dot_general(q_state[sl], st.astype(BF16), nt, preferred_element_type=F32)
        upd = lax.dot_general(v16[sl], k_state[sl], tn, preferred_element_type=F32)
        st_ref[b] = st * jnp.exp(g_end[b]) + jnp.where(bd, upd, 0.0)
        outs.append(o_b)
    o_ref[...] = (out + jnp.concatenate(outs, axis=0)).reshape(bn, HGRN_CHUNK, ch).astype(o_ref.dtype)


def _hgrn(u, lb, reverse):
    bn, ltot, _ = u.shape
    n_chunks = ltot // HGRN_CHUNK
    n_ctx = CTX_LEN // HGRN_CHUNK

    def chunk(i):
        if not reverse:
            return i
        return jnp.where(i < n_ctx, n_ctx - 1 - i, n_chunks - 1 + n_ctx - i)

    col = lambda j: (lambda i: (0, chunk(i), j))
    return pl.pallas_call(
        functools.partial(_hgrn_kernel, reverse=reverse),
        grid=(n_chunks,),
        in_specs=[
            pl.BlockSpec((bn, HGRN_CHUNK, GROUP_W), col(2)),
            pl.BlockSpec((bn, HGRN_CHUNK, GROUP_W), col(3)),
            pl.BlockSpec((bn, HGRN_CHUNK, GROUP_W), col(5 if reverse else 4)),
            pl.BlockSpec((1, GROUP_W), lambda i: (0, 0)),
        ],
        out_specs=pl.BlockSpec((bn, HGRN_CHUNK, GROUP_W), lambda i: (0, chunk(i), 0)),
        out_shape=jax.ShapeDtypeStruct((bn, ltot, GROUP_W), MIX_DTYPE),
        scratch_shapes=[pltpu.VMEM((bn, GROUP_W, GROUP_W), F32)],
        name="hgrn2_bwd" if reverse else "hgrn2_fwd",
    )(u, u, u, lb)


def _attn_kernel(lamp_ref, g_ref, qt_ref, k_ref, vt_ref, o_ref, s_scr, *, lam_init, q_start, nk):
    tq = TOK_TILE
    n_sub = o_ref.shape[1] // tq

    lp = lamp_ref[...]
    lam = (jnp.exp(jnp.sum(lp[0:1] * lp[1:2], axis=-1, keepdims=True))
           - jnp.exp(jnp.sum(lp[2:3] * lp[3:4], axis=-1, keepdims=True)) + lam_init)
    sub = _div_pow2(lax.broadcasted_iota(jnp.int32, (GROUP_W, tq), 0), SUB_HEAD_W)

    part = nk // ATT_KEY_PARTS
    parts = [slice(c * part, (c + 1) * part) for c in range(ATT_KEY_PARTS)]

    def pair_weights(i, j):
        qt = qt_ref[0, :, pl.ds(pl.multiple_of(q_start + i * tq, tq), tq)]
        return jnp.concatenate([jnp.where(sub == 2 * j, qt, jnp.zeros_like(qt)),
                                jnp.where(sub == 2 * j + 1, qt, jnp.zeros_like(qt))], axis=1)

    def scores_part(w, j, rows):
        s = jnp.dot(k_ref[0, rows, :], w, preferred_element_type=F32)
        s_scr[j % 2, rows, :] = s
        return jnp.max(s, axis=0, keepdims=True)

    def mix_part(j, m, rows):
        p = jnp.exp2(s_scr[j % 2, rows, :] - m).astype(BF16)
        lhs = jnp.concatenate([vt_ref[0, j * HEAD_W:(j + 1) * HEAD_W, rows],
                               jnp.ones((ATT_SUM_ROWS, part), BF16)], axis=0)
        return jnp.dot(lhs, p, preferred_element_type=F32)

    def first_scores():
        w = pair_weights(0, 0)
        return functools.reduce(jnp.maximum, [scores_part(w, 0, rows) for rows in parts])

    def tile(i, m):
        heads = []
        for j in range(N_PAIRS):
            nxt = (i, j + 1) if j + 1 < N_PAIRS else (jnp.minimum(i + 1, n_sub - 1), 0)
            w = pair_weights(*nxt)
            m_next, r = None, None
            for rows in parts:
                m_part = scores_part(w, (j + 1) % N_PAIRS, rows)
                m_next = m_part if m_next is None else jnp.maximum(m_next, m_part)
                r_part = mix_part(j, m, rows)
                r = r_part if r is None else r + r_part
            o, l = r[:HEAD_W], r[HEAD_W:HEAD_W + 1]
            heads.append(o[:, :tq] / l[:, :tq] - lam * (o[:, tq:] / l[:, tq:]))
            m = m_next
        ot = jnp.stack(heads)
        ms = jnp.mean(ot * ot, axis=1, keepdims=True)
        y = ot * lax.rsqrt(ms + RMS_EPS) * g_ref[...].reshape(1, HEAD_W, 1) * (1.0 - lam_init)
        o_ref[0, pl.ds(pl.multiple_of(i * tq, tq), tq), :] = y.reshape(GROUP_W, tq).T.astype(o_ref.dtype)
        return m

    lax.fori_loop(0, n_sub, tile, first_scores())


def _attention(lam_params, norm_g, q_t, k_rot, v_t, lam_init, q_start, n_q, nk):
    bsz, ltot, _ = k_rot.shape
    return pl.pallas_call(
        functools.partial(_attn_kernel, lam_init=lam_init, q_start=q_start, nk=nk),
        grid=(bsz,),
        in_specs=[
            pl.BlockSpec(lam_params.shape, lambda b: (0, 0)),
            pl.BlockSpec(norm_g.shape, lambda b: (0, 0)),
            pl.BlockSpec((1, GROUP_W, ltot), lambda b: (b, 0, 0)),
            pl.BlockSpec((1, ltot, GROUP_W), lambda b: (b, 0, 0)),
            pl.BlockSpec((1, GROUP_W, ltot), lambda b: (b, 0, 0)),
        ],
        out_specs=pl.BlockSpec((1, n_q, GROUP_W), lambda b: (b, 0, 0)),
        out_shape=jax.ShapeDtypeStruct((bsz, n_q, GROUP_W), MIX_DTYPE),
        scratch_shapes=[pltpu.VMEM((2, nk, 2 * TOK_TILE), F32)],
        name="diff_attention",
    )(lam_params, norm_g, q_t, k_rot, v_t)


def _outproj_kernel(x_ref, mod_ref, gpost_ref, hgain_ref, w_ref, ga_ref, gb_ref, gc_ref, gd_ref,
                    ya_ref, ybf_ref, ybb_ref, yc_ref, yd_ref, ydc_ref, o_ref, *, ctx_row, t_off):
    b = pl.program_id(0)
    t = pl.program_id(1) + t_off
    row = jnp.where(t == 0, ctx_row, b)
    gate = mod_ref[pl.ds(row, 1), :][:, 2 * D_MODEL:]
    ya = ya_ref[0, 0].astype(F32) + ya_ref[1, 0].astype(F32)
    yb = _head_rms_norm(ybf_ref[0].astype(F32) + ybb_ref[0].astype(F32), hgain_ref[...])
    yd = jnp.where(t == 0, ydc_ref[0], yd_ref[0]).astype(F32)
    groups = ((ya, ga_ref), (yb, gb_ref), (yc_ref[0].astype(F32), gc_ref), (yd, gd_ref))
    acc = jnp.zeros((x_ref.shape[1], D_MODEL), F32)
    for j, (y, g_ref) in enumerate(groups):
        m = (y * _silu(g_ref[0])).astype(BF16)
        acc = acc + jnp.dot(m, w_ref[j * GROUP_W:(j + 1) * GROUP_W, :], preferred_element_type=F32)
    r = acc * lax.rsqrt(jnp.mean(acc * acc, axis=-1, keepdims=True) + RMS_EPS) * gpost_ref[...]
    o_ref[0] = x_ref[0] + gate * r


def _outproj(xs, mod, g_post, hgrn_gain, w_out, u, ya2, yb_f, yb_b, yc, yd, yd_ctx, ctx_row, with_ctx):
    bsz, ltot, d = xs.shape
    t_off = 0 if with_ctx else CTX_LEN // TOK_TILE
    n_t = ltot // TOK_TILE - t_off
    col =lambda j: (lambda b, t: (b, t + t_off, j))
    tok = lambda b, t: (b, t + t_off, 0)
    two = lambda b, t: (0, b, t + t_off, 0)
    grp = pl.BlockSpec((1, TOK_TILE, GROUP_W), tok)
    return pl.pallas_call(
        functools.partial(_outproj_kernel, ctx_row=ctx_row, t_off=t_off),
        grid=(bsz, n_t),
        in_specs=[
            pl.BlockSpec((1, TOK_TILE, d), tok),
            pl.BlockSpec(mod.shape, lambda b, t: (0, 0)),
            pl.BlockSpec((1, d), lambda b, t: (0, 0)),
            pl.BlockSpec((1, GROUP_W), lambda b, t: (0, 0)),
            pl.BlockSpec(w_out.shape, lambda b, t: (0, 0)),
            pl.BlockSpec((1, TOK_TILE, GROUP_W), col(U_LRU_GATE)),
            pl.BlockSpec((1, TOK_TILE, GROUP_W), col(U_HGRN_GATE)),
            pl.BlockSpec((1, TOK_TILE, GROUP_W), col(U_CONF_GATE)),
            pl.BlockSpec((1, TOK_TILE, GROUP_W), col(U_DIFF_GATE)),
            pl.BlockSpec((2, 1, TOK_TILE, GROUP_W), two),
            grp, grp, grp,
            pl.BlockSpec((1, TOK_TILE, GROUP_W), lambda b, t: (b, jnp.maximum(t + t_off - 1, 0), 0)),
            pl.BlockSpec((1, TOK_TILE, GROUP_W), lambda b, t: (b, 0, 0)),
        ],
        out_specs=pl.BlockSpec((1, TOK_TILE, d), lambda b, t: (b, t, 0)),
        out_shape=jax.ShapeDtypeStruct((bsz, n_t * TOK_TILE, d), F32),
        name="out_projection",
    )(xs, mod, g_post, hgrn_gain, w_out, u, u, u, u, ya2, yb_f, yb_b, yc, yd, yd_ctx)


def _rope_tables(n_latent):
    rows = n_latent // GRID_W
    row = jnp.repeat(jnp.arange(rows, dtype=F32), GRID_W)
    colp = jnp.tile(jnp.arange(GRID_W, dtype=F32), rows)
    n_freq = SUB_HEAD_W // 4
    inv_freq = ROPE_THETA ** (-jnp.arange(n_freq, dtype=F32) / n_freq)
    ang = jnp.concatenate([row[:, None] * inv_freq, colp[:, None] * inv_freq], axis=-1)
    cos, sin = jnp.cos(ang), jnp.sin(ang)
    n_sub = GROUP_W // SUB_HEAD_W
    cos_t = jnp.tile(jnp.concatenate([cos, cos], axis=-1), (1, n_sub))
    sin_t = jnp.tile(jnp.concatenate([-sin, sin], axis=-1), (1, n_sub))
    cos_t = jnp.concatenate([jnp.ones((CTX_LEN, GROUP_W), F32), cos_t], axis=0)
    sin_t = jnp.concatenate([jnp.zeros((CTX_LEN, GROUP_W), F32), sin_t], axis=0)
    return cos_t, sin_t


def _block_diag(w):
    return jax.scipy.linalg.block_diag(*[w[h] for h in range(w.shape[0])])


def kernel(x, c, ctx, c_ctx, w_mod, b_mod, g_pre, g_post, w_in, w_out, lru_conv_w, lru_conv_b, lru_w_r, lru_b_r, lru_w_i, lru_b_i, lru_lambda, hgrn_lb, hgrn_norm_g, conf_conv_w, conf_conv_b, conf_ln_g, conf_ln_b, diff_lam_q1, diff_lam_k1, diff_lam_q2, diff_lam_k2, diff_norm_g):
    bsz, n_latent, d = x.shape
    depth = w_mod.shape[0]
    assert d == D_MODEL and ctx.shape[1] == CTX_LEN and n_latent % TOK_TILE == 0
    assert CTX_LEN == TOK_TILE

    ctx_row = bsz
    mod_rows = -(-(bsz + 1) // SUBLANES) * SUBLANES
    cc = jnp.zeros((mod_rows, d), F32).at[:bsz].set(c).at[ctx_row].set(c_ctx)
    mod = _modulation(cc, w_mod, b_mod[:, None, :])
    lbs = _lower_bounds(hgrn_lb.reshape(depth, 2 * GROUP_W)).reshape(depth, 2, 1, GROUP_W)
    cos_t, sin_t = _rope_tables(n_latent)

    xs = jnp.concatenate([ctx, x], axis=1)
    for layer in range(depth):
        with_ctx = layer < depth - 1
        w16 = w_in[layer].astype(BF16)
        w_main = jnp.concatenate([w16[:, :QKV_FIRST * GROUP_W], w16[:, (QKV_FIRST + 3) * GROUP_W:]], axis=1)
        w_qkv = w16[:, QKV_FIRST * GROUP_W:(QKV_FIRST + 3) * GROUP_W]
        u, q_t, k_rot, v_t = _inproj(xs, mod[layer], g_pre[layer][None], w_main, w_qkv, cos_t, sin_t, ctx_row)

        wg = jnp.stack([jnp.concatenate([_block_diag(lru_w_r[layer, dd]), _block_diag(lru_w_i[layer, dd])], axis=1)
                        for dd in range(2)]).astype(BF16)
        bg = jnp.concatenate([lru_b_r[layer], lru_b_i[layer]], axis=-1)[:, None, :]
        ya2 = _lru(u, lru_conv_w[layer], lru_conv_b[layer][None], wg, bg, lru_lambda[layer][:, None, :])

        yc = _conformer(u, conf_conv_w[layer], conf_conv_b[layer][None], conf_ln_g[layer][None],
                        conf_ln_b[layer][None])

        yb_f = _hgrn(u, lbs[layer, 0], reverse=False)
        yb_b = _hgrn(u, lbs[layer, 1], reverse=True)

        lam_init = 0.8 - 0.6 * math.exp(-0.3 * layer)
        lam_params = jnp.stack([diff_lam_q1[layer], diff_lam_k1[layer], diff_lam_q2[layer], diff_lam_k2[layer]])
        att = functools.partial(_attention, lam_params, diff_norm_g[layer][:, None], q_t, k_rot, v_t, lam_init)
        yd = att(CTX_LEN, n_latent, CTX_LEN + n_latent)
        yd_ctx = att(0, CTX_LEN, CTX_LEN) if with_ctx else yd

        xs = _outproj(xs, mod[layer], g_post[layer][None], jnp.tile(hgrn_norm_g[layer], N_PAIRS)[None],
                      w_out[layer].astype(BF16), u, ya2, yb_f, yb_b, yc, yd, yd_ctx, ctx_row, with_ctx)
    return xs
```
